```python
import math
import jax
import jax.numpy as jnp
from jax import lax
import numpy as np

D_MODEL = 1024
BATCH = 32
SEQ = 2048
DEPTH = 4

CHUNK = 64
Q_BLOCK = 128
HEAD_DIM = 64
N_HEADS = D_MODEL // HEAD_DIM
KV_DIM = 64
IDX_HEADS = 8
IDX_DIM = 64
INDEX_TOPK = 256
RNN_WIDTH = D_MODEL
RNN_BLOCKS = 8
RNN_BW = RNN_WIDTH // RNN_BLOCKS
CONV_W = 4
LRU_C = 8.0
REL_BUCKETS = 32
REL_MAX_DIST = 128
D_FF = 2816
N_EXPERTS = 8
TOP_K = 2
D_FF_EXPERT = 3584
PLE_DIM = 256
LN_EPS = 1e-5
DN_ALPHA = (2 * DEPTH) ** 0.25
DN_BETA = (8 * DEPTH) ** -0.25
SPLITS = (N_HEADS * HEAD_DIM, KV_DIM, KV_DIM, IDX_HEADS * IDX_DIM, IDX_DIM, IDX_HEADS, RNN_WIDTH, RNN_WIDTH, D_MODEL, D_MODEL)
D_IN = sum(SPLITS)
SPLIT_OFFSETS = tuple(int(o) for o in np.cumsum(SPLITS)[:-1])
N_DENSE = (DEPTH + 1) // 2
N_MOE = DEPTH // 2

kernel_name = 'hybrid_dsa_rglru_moe_deepnorm'


def layer_norm(x, g, b):
    xf = x.astype(jnp.float32)
    mu = jnp.mean(xf, axis=-1, keepdims=True)
    var = jnp.mean(jnp.square(xf - mu), axis=-1, keepdims=True)
    return ((xf - mu) * lax.rsqrt(var + LN_EPS)).astype(x.dtype) * g + b


def t5_bucket(rel):
    half = REL_BUCKETS // 2
    max_exact = half // 2
    ret = jnp.where(rel > 0, half, 0)
    n = jnp.abs(rel)
    nf = jnp.maximum(n, 1).astype(jnp.float32)
    large = max_exact + (jnp.log(nf / max_exact) / math.log(REL_MAX_DIST / max_exact) * (half - max_exact)).astype(jnp.int32)
    large = jnp.minimum(large, half - 1)
    return ret + jnp.where(n < max_exact, n, large)


def dsa_attention(q, k, v, q_idx, k_idx, w_idx, rel_bias):
    b, s, h, dh = q.shape
    k_top = min(INDEX_TOPK, s // 4)
    nb = s // Q_BLOCK
    key_chunk = jnp.arange(s) // CHUNK

    def to_blocks(t):
        return jnp.moveaxis(t.reshape(b, nb, Q_BLOCK, *t.shape[2:]), 1, 0)

    def block(args):
        qb, qib, wb, start = args
        qpos = start + jnp.arange(Q_BLOCK)
        qchunk = qpos // CHUNK
        admiss = key_chunk[None, :] <= qchunk[:, None]
        dots = jnp.einsum('bqhd,bsd->bqhs', qib, k_idx).astype(jnp.float32) * (IDX_DIM ** -0.5)
        score = jnp.einsum('bqhs,bqh->bqs', jax.nn.relu(dots), wb.astype(jnp.float32)) * (IDX_HEADS ** -0.5)
        score = jnp.where(admiss[None], score, -jnp.inf)
        _, idx = lax.top_k(score, k_top)
        kg = jax.vmap(lambda kk, ii: kk[ii])(k, idx)
        vg = jax.vmap(lambda vv, ii: vv[ii])(v, idx)
        logits = jnp.einsum('bqhd,bqkd->bqhk', qb, kg).astype(jnp.float32) * (dh ** -0.5)
        bias = rel_bias[t5_bucket(idx - qpos[None, :, None])]
        logits = logits + jnp.moveaxis(bias, -1, 2).astype(jnp.float32)
        valid = (idx // CHUNK) <= qchunk[None, :, None]
        logits = jnp.where(valid[:, :, None, :], logits, -1e30)
        probs = jax.nn.softmax(logits, axis=-1).astype(v.dtype)
        return jnp.einsum('bqhk,bqkd->bqhd', probs, vg)

    starts = jnp.arange(nb, dtype=jnp.int32) * Q_BLOCK
    out = lax.map(block, (to_blocks(q), to_blocks(q_idx), to_blocks(w_idx), starts))
    return jnp.moveaxis(out, 0, 1).reshape(b, s, h * dh)


def causal_conv(x, w, bias):
    s = x.shape[1]
    xp = jnp.pad(x, ((0, 0), (CONV_W - 1, 0), (0, 0)))
    out = xp[:, 0:s] * w[0]
    for j in range(1, CONV_W):
        out = out + xp[:, j:j + s] * w[j]
    return out + bias


def rg_lru(xc, wa, ba, wx, bx, lam):
    b, s, d = xc.shape
    xb = xc.reshape(b, s, RNN_BLOCKS, RNN_BW)
    r = jax.nn.sigmoid(jnp.einsum('bsnc,ncd->bsnd', xb, wa).reshape(b, s, d) + ba)
    i = jax.nn.sigmoid(jnp.einsum('bsnc,ncd->bsnd', xb, wx).reshape(b, s, d) + bx)
    log_a = -LRU_C * r.astype(jnp.float32) * jax.nn.softplus(-lam.astype(jnp.float32))
    a = jnp.exp(log_a)
    u = jnp.sqrt(-jnp.expm1(2.0 * log_a)) * (i * xc).astype(jnp.float32)

    def combine(c1, c2):
        a1, b1 = c1
        a2, b2 = c2
        return a1 * a2, a2 * b1 + b2

    _, hs = lax.associative_scan(combine, (a, u), axis=1)
    return hs.astype(xc.dtype)


def token_mixer(x, w_in, conv_w, conv_b, lru_wa, lru_ba, lru_wx, lru_bx, lru_lam, w_br_attn, w_br_rnn, w_o, rel_bias):
    b, s, _ = x.shape
    z = x @ w_in
    q, k, v, qi, ki, wi, xr, yr, ga, gr = jnp.split(z, SPLIT_OFFSETS, axis=-1)
    q = q.reshape(b, s, N_HEADS, HEAD_DIM)
    qi = qi.reshape(b, s, IDX_HEADS, IDX_DIM)
    o_attn = dsa_attention(q, k, v, qi, ki, wi, rel_bias)
    xc = causal_conv(xr, conv_w, conv_b)
    o_rnn = rg_lru(xc, lru_wa, lru_ba, lru_wx, lru_bx, lru_lam) * jax.nn.gelu(yr)
    merged = jax.nn.sigmoid(ga) * (o_attn @ w_br_attn) + jax.nn.sigmoid(gr) * (o_rnn @ w_br_rnn)
    return merged @ w_o


def swiglu(x, w_gate, w_up, w_down):
    return (jax.nn.silu(x @ w_gate) * (x @ w_up)) @ w_down


def moe_swiglu(x, router, router_b, w_gate, w_up, w_down):
    logits = (x @ router).astype(jnp.float32) + router_b.astype(jnp.float32)
    top_v, top_i = lax.top_k(logits, TOP_K)
    gates = jax.nn.softmax(top_v, axis=-1)
    combine = jnp.sum(jax.nn.one_hot(top_i, N_EXPERTS, dtype=jnp.float32) * gates[..., None], axis=-2).astype(x.dtype)
    out = jnp.zeros_like(x)
    for e in range(N_EXPERTS):
        out = out + combine[..., e:e + 1] * swiglu(x, w_gate[e], w_up[e], w_down[e])
    return out


def setup_inputs(seed: int = 0) -> dict:
    key = jax.random.key(seed)
    ks = jax.random.split(key, 28)
    f32 = jnp.float32

    def nrm(k, shape, scale):
        return jax.random.normal(k, shape, f32) * scale

    a0 = jax.random.uniform(ks[9], (DEPTH, RNN_WIDTH), f32, 0.9, 0.999)
    return {
        'x': nrm(ks[0], (BATCH, SEQ, D_MODEL), 1.0),
        'p': nrm(ks[1], (DEPTH, BATCH, SEQ, PLE_DIM), 1.0),
        'w_in': nrm(ks[2], (DEPTH, D_MODEL, D_IN), D_MODEL ** -0.5),
        'conv_w': nrm(ks[3], (DEPTH, CONV_W, RNN_WIDTH), CONV_W ** -0.5),
        'conv_b': nrm(ks[4], (DEPTH, RNN_WIDTH), 0.01),
        'lru_wa': nrm(ks[5], (DEPTH, RNN_BLOCKS, RNN_BW, RNN_BW), RNN_BW ** -0.5),
        'lru_ba': nrm(ks[6], (DEPTH, RNN_WIDTH), 0.01),
        'lru_wx': nrm(ks[7], (DEPTH, RNN_BLOCKS, RNN_BW, RNN_BW), RNN_BW ** -0.5),
        'lru_bx': nrm(ks[8], (DEPTH, RNN_WIDTH), 0.01),
        'lru_lam': jnp.log(a0) - jnp.log1p(-a0),
        'w_br_attn': nrm(ks[10], (DEPTH, N_HEADS * HEAD_DIM, D_MODEL), (N_HEADS * HEAD_DIM) ** -0.5),
        'w_br_rnn': nrm(ks[11], (DEPTH, RNN_WIDTH, D_MODEL), RNN_WIDTH ** -0.5),
        'w_o': nrm(ks[12], (DEPTH, D_MODEL, D_MODEL), DN_BETA * D_MODEL ** -0.5),
        'rel_bias': nrm(ks[13], (REL_BUCKETS, N_HEADS), 0.5),
        'ln1_g': 1.0 + nrm(ks[14], (DEPTH, D_MODEL), 0.02),
        'ln1_b': nrm(ks[15], (DEPTH, D_MODEL), 0.01),
        'ffn_w_gate': nrm(ks[16], (N_DENSE, D_MODEL, D_FF), D_MODEL ** -0.5),
        'ffn_w_up': nrm(ks[17], (N_DENSE, D_MODEL, D_FF), D_MODEL ** -0.5),
        'ffn_w_down': nrm(ks[18], (N_DENSE, D_FF, D_MODEL), DN_BETA * D_FF ** -0.5),
        'moe_router': nrm(ks[19], (N_MOE, D_MODEL, N_EXPERTS), D_MODEL ** -0.5),
        'moe_router_b': nrm(ks[20], (N_MOE, N_EXPERTS), 0.01),
        'moe_w_gate': nrm(ks[21], (N_MOE, N_EXPERTS, D_MODEL, D_FF_EXPERT), D_MODEL ** -0.5),
        'moe_w_up': nrm(ks[22], (N_MOE, N_EXPERTS, D_MODEL, D_FF_EXPERT), D_MODEL ** -0.5),
        'moe_w_down': nrm(ks[23], (N_MOE, N_EXPERTS, D_FF_EXPERT, D_MODEL), DN_BETA * D_FF_EXPERT ** -0.5),
        'ple_w_gate': nrm(ks[24], (DEPTH, D_MODEL, D_MODEL), D_MODEL ** -0.5),
        'ple_w_proj': nrm(ks[25], (DEPTH, PLE_DIM, D_MODEL), DN_BETA * PLE_DIM ** -0.5),
        'ln2_g': 1.0 + nrm(ks[26], (DEPTH, D_MODEL), 0.02),
        'ln2_b': nrm(ks[27], (DEPTH, D_MODEL), 0.01),
    }


def reference(x, p, w_in, conv_w, conv_b, lru_wa, lru_ba, lru_wx, lru_bx, lru_lam, w_br_attn, w_br_rnn, w_o, rel_bias, ln1_g, ln1_b, ffn_w_gate, ffn_w_up, ffn_w_down, moe_router, moe_router_b, moe_w_gate, moe_w_up, moe_w_down, ple_w_gate, ple_w_proj, ln2_g, ln2_b):
    for i in range(DEPTH):
        mix = token_mixer(x, w_in[i], conv_w[i], conv_b[i], lru_wa[i], lru_ba[i], lru_wx[i], lru_bx[i], lru_lam[i], w_br_attn[i], w_br_rnn[i], w_o[i], rel_bias)
        x = layer_norm(DN_ALPHA * x + mix, ln1_g[i], ln1_b[i])
        j = i // 2
        if i % 2 == 0:
            f = swiglu(x, ffn_w_gate[j], ffn_w_up[j], ffn_w_down[j])
        else:
            f = moe_swiglu(x, moe_router[j], moe_router_b[j], moe_w_gate[j], moe_w_up[j], moe_w_down[j])
        ple = jax.nn.sigmoid(x @ ple_w_gate[i]) * (p[i] @ ple_w_proj[i])
        x = layer_norm(DN_ALPHA * x + f + ple, ln2_g[i], ln2_b[i])
    return x
```

```python
import functools
import math

import jax
import jax.numpy as jnp
import numpy as np
from jax import lax
from jax.experimental import pallas as pl
from jax.experimental.pallas import tpu as pltpu

CHUNK = 64
Q_BLOCK = 128
HEAD_DIM = 64
KV_DIM = 64
IDX_HEADS = 8
IDX_DIM = 64
INDEX_TOPK = 256
RNN_BLOCKS = 8
CONV_W = 4
LRU_C = 8.0
REL_BUCKETS = 32
REL_MAX_DIST = 128
N_EXPERTS = 8
TOP_K = 2
LN_EPS = 1e-5

LANES = 128
SUBLANES = 8
VMEM_LIMIT_BYTES = 56 * 1024 * 1024

KEY_TILE = Q_BLOCK
INT_MIN = -(2 ** 31)
NEG_BIG = -1e30
LOG2E = math.log2(math.e)

BF16 = jnp.bfloat16
F32 = jnp.float32
I32 = jnp.int32

_NT_DIMS = (((1,), (1,)), ((), ()))


def _params(n_axes):
    return pltpu.CompilerParams(
        dimension_semantics=("arbitrary",) * n_axes,
        vmem_limit_bytes=VMEM_LIMIT_BYTES)


def _mm_kernel(x_ref, w_ref, s_ref, o_ref):
    acc = jnp.dot(x_ref[...], w_ref[...], preferred_element_type=F32)
    o_ref[...] = (acc * s_ref[...]).astype(o_ref.dtype)


def _matmul(x, w, scale, out_dtype, tm, tn):
    m, k = x.shape
    n = w.shape[1]
    return pl.pallas_call(
        _mm_kernel,
        grid=(m // tm, n // tn),
        in_specs=[pl.BlockSpec((tm, k), lambda i, j: (i, 0)),
                  pl.BlockSpec((k, tn), lambda i, j: (0, j)),
                  pl.BlockSpec((1, tn), lambda i, j: (0, j))],
        out_specs=pl.BlockSpec((tm, tn), lambda i, j: (i, j)),
        out_shape=jax.ShapeDtypeStruct((m, n), out_dtype),
        compiler_params=_params(2),
        name="proj_in",
    )(x, w, scale)


def _sum_rows_to_8(x):
    acc = x[0:SUBLANES, :]
    for r in range(1, x.shape[0] // SUBLANES):
        acc = acc + x[r * SUBLANES:(r + 1) * SUBLANES, :]
    return acc


def _attn_kernel(fb_ref, zs_ref, q_ref, qi_ref, nb_ref, o_ref,
                 k_s, ki_s, vt_s, skey_s, l_s, ot_s, *, k_top, n_heads, pos_bits):
    i = pl.program_id(1)
    n_tiles = i + 1
    start = pl.multiple_of(i * Q_BLOCK, Q_BLOCK)

    @pl.when(i == 0)
    def _():
        kv = zs_ref[:, 0:2 * KV_DIM]
        k_s[...] = kv[:, 0:KV_DIM].astype(BF16)
        vt_s[...] = kv.T[KV_DIM:2 * KV_DIM, :].astype(BF16)
        ki_s[...] = zs_ref[:, 2 * KV_DIM:2 * KV_DIM + IDX_DIM].astype(BF16)

    w_t = zs_ref[pl.ds(start, Q_BLOCK), LANES:2 * LANES].T

    row = lax.broadcasted_iota(I32, (KEY_TILE, Q_BLOCK), 0)
    lane = lax.broadcasted_iota(I32, (KEY_TILE, Q_BLOCK), 1)
    q_chunk = (start + lane) // CHUNK

    def score_tile(kt, carry):
        off = pl.multiple_of(kt * KEY_TILE, KEY_TILE)
        ki_t = ki_s[pl.ds(off, KEY_TILE), :]
        acc = jnp.zeros((KEY_TILE, Q_BLOCK), F32)
        for h in range(IDX_HEADS):
            qi_h = qi_ref[:, h * IDX_DIM:(h + 1) * IDX_DIM]
            d = lax.dot_general(ki_t, qi_h, _NT_DIMS, preferred_element_type=F32)
            acc = acc + jnp.maximum(d, 0.0) * w_t[IDX_DIM + h:IDX_DIM + h + 1, :]
        bits = lax.bitcast_convert_type(acc, I32)
        key = bits ^ ((bits >> 31) & jnp.int32(0x7FFFFFFF))
        k_chunk = (off + row) // CHUNK
        skey_s[pl.ds(off, KEY_TILE), :] = jnp.where(k_chunk <= q_chunk, key, INT_MIN)
        return carry

    lax.fori_loop(0, n_tiles, score_tile, 0)

    def count(pred):
        def body(kt, c):
            off = pl.multiple_of(kt * KEY_TILE, KEY_TILE)
            hit = pred(skey_s[pl.ds(off, KEY_TILE), :], off + row)
            return c + _sum_rows_to_8(jnp.where(hit, 1, 0).astype(I32))
        c = lax.fori_loop(0, n_tiles, body, jnp.zeros((SUBLANES, Q_BLOCK), I32))
        return jnp.sum(c, axis=0, keepdims=True)

    def bisect(it, t):
        cand = t ^ lax.shift_left(jnp.int32(1), 31 - it)
        c = count(lambda key, pos: key >= cand)
        return jnp.where(c >= k_top, cand, t)

    t = lax.fori_loop(0, 32, bisect, jnp.full((1, Q_BLOCK), INT_MIN, I32))

    n_ge = count(lambda key, pos: key >= t)

    @pl.when(jnp.max(n_ge) > k_top)
    def _():
        n_gt = count(lambda key, pos: key > t)
        r_m1 = k_top - n_gt - 1

        def pos_search(it, j):
            cand = j | lax.shift_left(jnp.int32(1), pos_bits - 1 - it)
            c = count(lambda key, pos: (key == t) & (pos < cand))
            return jnp.where(c <= r_m1, cand, j)

        j_cut = lax.fori_loop(0, pos_bits, pos_search, jnp.zeros((1, Q_BLOCK), I32))

        def drop(kt, carry):
            off = pl.multiple_of(kt * KEY_TILE, KEY_TILE)
            key = skey_s[pl.ds(off, KEY_TILE), :]
            cut = (key == t) & ((off + row) > j_cut)
            skey_s[pl.ds(off, KEY_TILE), :] = jnp.where(cut, INT_MIN, key)
            return carry

        lax.fori_loop(0, n_tiles, drop, 0)

    t_sel = jnp.maximum(t, INT_MIN + 1)

    n_far = jnp.maximum(i - 1, 0)
    for h in range(n_heads):
        q_h = q_ref[:, h * HEAD_DIM:(h + 1) * HEAD_DIM]
        far_bias = fb_ref[h]

        def logits_tile(kt, bias):
            off = pl.multiple_of(kt * KEY_TILE, KEY_TILE)
            lg = lax.dot_general(k_s[pl.ds(off, KEY_TILE), :], q_h, _NT_DIMS,
                                 preferred_element_type=F32) + bias
            lg = jnp.where(skey_s[pl.ds(off, KEY_TILE), :] >= t_sel, lg, NEG_BIG)
            l_s[pl.ds(off, KEY_TILE), :] = lg
            return lg

        def far_pass(kt, m_run):
            return jnp.maximum(m_run, logits_tile(kt, far_bias))

        def near_pass(kt, m_run):
            nb_off = pl.multiple_of((kt - (i - 1)) * KEY_TILE, KEY_TILE)
            return jnp.maximum(m_run, logits_tile(kt, nb_ref[h, pl.ds(nb_off, KEY_TILE), :]))

        m_run = jnp.full((KEY_TILE, Q_BLOCK), NEG_BIG, F32)
        m_run = lax.fori_loop(0, n_far, far_pass, m_run)
        m_run = lax.fori_loop(n_far, n_tiles, near_pass, m_run)
        m = jnp.max(m_run, axis=0, keepdims=True)

        def pv_pass(kt, carry):
            l_run, acc = carry
            off = pl.multiple_of(kt * KEY_TILE, KEY_TILE)
            p = jnp.exp2(l_s[pl.ds(off, KEY_TILE), :] - m)
            acc = acc + jnp.dot(vt_s[:, pl.ds(off, KEY_TILE)], p.astype(BF16),
                                preferred_element_type=F32)
            return l_run + p, acc

        l_run, acc = lax.fori_loop(
            0, n_tiles, pv_pass,
            (jnp.zeros((KEY_TILE, Q_BLOCK), F32), jnp.zeros((HEAD_DIM, Q_BLOCK), F32)))
        denom = jnp.sum(l_run, axis=0, keepdims=True)
        ot_s[h * HEAD_DIM:(h + 1) * HEAD_DIM, :] = acc / denom

    o_ref[...] = ot_s[...].T.astype(o_ref.dtype)


def _attention(zs, zb, near_bias, far_bias, batch, seq, n_heads, qi_block):
    d_q = n_heads * HEAD_DIM
    nqb = seq // Q_BLOCK
    k_top = min(INDEX_TOPK, seq // 4)
    kern = functools.partial(_attn_kernel, k_top=k_top, n_heads=n_heads,
                             pos_bits=max(1, (seq - 1).bit_length()))
    grid_spec = pltpu.PrefetchScalarGridSpec(
        num_scalar_prefetch=1,
        grid=(batch, nqb),
        in_specs=[
            pl.BlockSpec((seq, 2 * LANES), lambda b, i, fb: (b, 0)),
            pl.BlockSpec((Q_BLOCK, d_q), lambda b, i, fb: (b * nqb + i, 0)),
            pl.BlockSpec((Q_BLOCK, IDX_HEADS * IDX_DIM), lambda b, i, fb: (b * nqb + i, qi_block)),
            pl.BlockSpec((n_heads, 2 * KEY_TILE, Q_BLOCK), lambda b, i, fb: (0, 0, 0)),
        ],
        out_specs=pl.BlockSpec((Q_BLOCK, d_q), lambda b, i, fb: (b * nqb + i, 0)),
        scratch_shapes=[
            pltpu.VMEM((seq, KV_DIM), BF16),
            pltpu.VMEM((seq, IDX_DIM), BF16),
            pltpu.VMEM((KV_DIM, seq), BF16),
            pltpu.VMEM((seq, Q_BLOCK), I32),
            pltpu.VMEM((seq, Q_BLOCK), F32),
            pltpu.VMEM((d_q, Q_BLOCK), F32),
        ])
    return pl.pallas_call(
        kern, grid_spec=grid_spec,
        out_shape=jax.ShapeDtypeStruct((batch * seq, d_q), BF16),
        compiler_params=_params(2),
        name="dsa_attention",
    )(far_bias, zs, zb, zb, near_bias)


def _t5_bucket(rel):
    half = REL_BUCKETS // 2
    max_exact = half // 2
    ret = jnp.where(rel > 0, half, 0)
    n = jnp.abs(rel)
    nf = jnp.maximum(n, 1).astype(F32)
    large = max_exact + (jnp.log(nf / max_exact) / math.log(REL_MAX_DIST / max_exact)
                         * (half - max_exact)).astype(I32)
    large = jnp.minimum(large, half - 1)
    return ret + jnp.where(n < max_exact, n, large)


def _bias_tables(rel_bias):
    kk = np.arange(2 * KEY_TILE)[:, None] - KEY_TILE
    qq = np.arange(Q_BLOCK)[None, :]
    bucket = _t5_bucket(jnp.asarray(kk - qq, I32))
    near = jnp.moveaxis(rel_bias[bucket], -1, 0).astype(F32) * LOG2E
    far_bucket = _t5_bucket(jnp.asarray([-(KEY_TILE + 1)], I32))[0]
    far = rel_bias[far_bucket].astype(F32) * LOG2E
    return near, far


def _neg_expm1(y):
    series = -y * (1.0 + y * (1.0 / 2 + y * (1.0 / 6 + y * (1.0 / 24 + y * (1.0 / 120)))))
    return jnp.where(y > -0.1, series, 1.0 - jnp.exp(y))


def _rglru_kernel(xr_ref, yr_ref, cw_ref, cb_ref, wa_ref, ba_ref, wx_ref, bx_ref, lam_ref,
                  o_ref, ext_s, a_s, u_s, h_s, *, ts, bw):
    j = pl.program_id(1)

    @pl.when(j == 0)
    def _():
        ext_s[0:SUBLANES, :] = jnp.zeros((SUBLANES, ext_s.shape[1]), F32)
        h_s[...] = jnp.zeros(h_s.shape, F32)

    x = xr_ref[...].astype(F32)
    ext_s[SUBLANES:SUBLANES + ts, :] = x
    xc = ext_s[SUBLANES - 3:SUBLANES - 3 + ts, :] * cw_ref[0:1, :]
    xc = xc + ext_s[SUBLANES - 2:SUBLANES - 2 + ts, :] * cw_ref[1:2, :]
    xc = xc + ext_s[SUBLANES - 1:SUBLANES - 1 + ts, :] * cw_ref[2:3, :]
    xc = xc + x * cw_ref[3:4, :]
    xc = xc + cb_ref[...]
    ext_s[0:SUBLANES, :] = ext_s[ts:ts + SUBLANES, :]

    sp = jax.nn.softplus(-lam_ref[...])
    for n in range(RNN_BLOCKS):
        cs = slice(n * bw, (n + 1) * bw)
        xb = xc[:, cs]
        xb16 = xb.astype(BF16)
        r = jax.nn.sigmoid(jnp.dot(xb16, wa_ref[n], preferred_element_type=F32) + ba_ref[:, cs])
        g = jax.nn.sigmoid(jnp.dot(xb16, wx_ref[n], preferred_element_type=F32) + bx_ref[:, cs])
        log_a = -LRU_C * r * sp[:, cs]
        a_s[:, cs] = jnp.exp(log_a)
        u_s[:, cs] = jnp.sqrt(_neg_expm1(2.0 * log_a)) * (g * xb)

    def step(t, h):
        h = a_s[pl.ds(t, 1), :] * h + u_s[pl.ds(t, 1), :]
        u_s[pl.ds(t, 1), :] = h
        return h

    h_s[...] = lax.fori_loop(0, ts, step, h_s[...], unroll=8)
    o_ref[...] = (u_s[...] * jax.nn.gelu(yr_ref[...].astype(F32))).astype(o_ref.dtype)


def _rglru(zb, conv_w, conv_b, wa, ba, wx, bx, lam, batch, seq, width, xr_block, yr_block, ts):
    nts = seq // ts
    bw = width // RNN_BLOCKS
    row = lambda v: v.reshape(1, width)
    full = lambda shape: pl.BlockSpec(shape, lambda b, j: (0,) * len(shape))
    return pl.pallas_call(
        functools.partial(_rglru_kernel, ts=ts, bw=bw),
        grid=(batch, nts),
        in_specs=[pl.BlockSpec((ts, width), lambda b, j: (b * nts + j, xr_block)),
                  pl.BlockSpec((ts, width), lambda b, j: (b * nts + j, yr_block)),
                  full((CONV_W, width)), full((1, width)),
                  full((RNN_BLOCKS, bw, bw)), full((1, width)),
                  full((RNN_BLOCKS, bw, bw)), full((1, width)), full((1, width))],
        out_specs=pl.BlockSpec((ts, width), lambda b, j: (b * nts + j, 0)),
        out_shape=jax.ShapeDtypeStruct((batch * seq, width), BF16),
        scratch_shapes=[pltpu.VMEM((ts + 2 * SUBLANES, width), F32),
                        pltpu.VMEM((ts, width), F32),
                        pltpu.VMEM((ts, width), F32),
                        pltpu.VMEM((1, width), F32)],
        compiler_params=_params(2),
        name="rglru",
    )(zb, zb, conv_w, row(conv_b), wa.astype(BF16), row(ba), wx.astype(BF16), row(bx), row(lam))


def _layer_norm(y, g, b):
    mu = jnp.mean(y, axis=-1, keepdims=True)
    var = jnp.mean(jnp.square(y - mu), axis=-1, keepdims=True)
    return (y - mu) * lax.rsqrt(var + LN_EPS) * g + b


def _merge_kernel(oa_ref, or_ref, ga_ref, gr_ref, x_ref, wba_ref, wbr_ref, wo_ref, g_ref, b_ref,
                  x1_ref, x1b_ref, *, alpha):
    a = jnp.dot(oa_ref[...], wba_ref[...], preferred_element_type=F32)
    r = jnp.dot(or_ref[...], wbr_ref[...], preferred_element_type=F32)
    merged = (jax.nn.sigmoid(ga_ref[...].astype(F32)) * a
              + jax.nn.sigmoid(gr_ref[...].astype(F32)) * r)
    mix = jnp.dot(merged.astype(BF16), wo_ref[...], preferred_element_type=F32)
    x1 = _layer_norm(alpha * x_ref[...] + mix, g_ref[...], b_ref[...])
    x1_ref[...] = x1
    x1b_ref[...] = x1.astype(BF16)


def _merge_ln(oa, orn, zb, x, wba, wbr, wo, g, b, alpha, ga_block, gr_block, tm):
    m, d = x.shape
    tok = lambda blk: pl.BlockSpec((tm, d), lambda i: (i, blk))
    full = lambda shape: pl.BlockSpec(shape, lambda i: (0,) * len(shape))
    return pl.pallas_call(
        functools.partial(_merge_kernel, alpha=alpha),
        grid=(m // tm,),
        in_specs=[tok(0), tok(0), tok(ga_block), tok(gr_block), tok(0),
                  full((d, d)), full((d, d)), full((d, d)), full((1, d)), full((1, d))],
        out_specs=[tok(0), tok(0)],
        out_shape=[jax.ShapeDtypeStruct((m, d), F32), jax.ShapeDtypeStruct((m, d), BF16)],
        compiler_params=_params(1),
        name="merge_ln1",
    )(oa, orn, zb, zb, x, wba, wbr, wo, g.reshape(1, d), b.reshape(1, d))


def _ple_kernel(*refs, alpha, routed):
    if routed:
        (x1_ref, x1b_ref, p_ref, pg_ref, pp_ref, g_ref, b_ref,
         ya_ref, yb_ref, gt_ref, x2_ref, x2b_ref) = refs
        f = (gt_ref[:, 0:1] * ya_ref[...].astype(F32)
             + gt_ref[:, 1:2] * yb_ref[...].astype(F32))
    else:
        x1_ref, x1b_ref, p_ref, pg_ref, pp_ref, g_ref, b_ref, f_ref, x2_ref, x2b_ref = refs
        f = f_ref[...].astype(F32)
    gate = jax.nn.sigmoid(jnp.dot(x1b_ref[...], pg_ref[...], preferred_element_type=F32))
    proj = jnp.dot(p_ref[...].astype(BF16), pp_ref[...], preferred_element_type=F32)
    x2 = _layer_norm(alpha * x1_ref[...] + f + gate * proj, g_ref[...], b_ref[...])
    x2_ref[...] = x2
    x2b_ref[...] = x2.astype(BF16)


def _ple_ln(x1, x1b, p_all, layer, pg, pp, g, b, f_parts, alpha, tm):
    m, d = x1.shape
    pd = p_all.shape[1]
    nt = m // tm
    tok = pl.BlockSpec((tm, d), lambda i: (i, 0))
    full = lambda shape: pl.BlockSpec(shape, lambda i: (0,) * len(shape))
    routed = len(f_parts) == 3
    f_specs = [tok, tok, pl.BlockSpec((tm, LANES), lambda i: (i, 0))] if routed else [tok]
    return pl.pallas_call(
        functools.partial(_ple_kernel, alpha=alpha, routed=routed),
        grid=(nt,),
        in_specs=[tok, tok, pl.BlockSpec((tm, pd), lambda i: (layer * nt + i, 0)),
                  full((d, d)), full((pd, d)), full((1, d)), full((1, d))] + f_specs,
        out_specs=[tok, tok],
        out_shape=[jax.ShapeDtypeStruct((m, d), F32), jax.ShapeDtypeStruct((m, d), BF16)],
        compiler_params=_params(1),
        name="ple_ln2",
    )(x1, x1b, p_all, pg, pp, g.reshape(1, d), b.reshape(1, d), *f_parts)


def _ffn_kernel(te_ref, nt_ref, x_ref, wg_ref, wu_ref, wd_ref, o_ref, acc_s):
    t = pl.program_id(0)
    j = pl.program_id(1)
    last = pl.num_programs(1) - 1
    live = t < nt_ref[0]

    @pl.when(live)
    def _():
        x = x_ref[...]
        gate = jnp.dot(x, wg_ref[0], preferred_element_type=F32)
        up = jnp.dot(x, wu_ref[0], preferred_element_type=F32)
        hid = (jax.nn.silu(gate) * up).astype(BF16)
        part = jnp.dot(hid, wd_ref[0], preferred_element_type=F32)

        @pl.when(j == 0)
        def _():
            acc_s[...] = part

        @pl.when(j > 0)
        def _():
            acc_s[...] += part

    @pl.when(j == last)
    def _():
        o_ref[...] = jnp.where(live, acc_s[...], 0.0).astype(o_ref.dtype)


def _ffn(x, tile_expert, n_live, wg, wu, wd, tm, tf):
    rows, d = x.shape
    f = wg.shape[2]
    fcol = lambda t, j, te, nt: jnp.where(t < nt[0], j, 0)
    grid_spec = pltpu.PrefetchScalarGridSpec(
        num_scalar_prefetch=2,
        grid=(rows // tm, f // tf),
        in_specs=[pl.BlockSpec((tm, d), lambda t, j, te, nt: (t, 0)),
                  pl.BlockSpec((1, d, tf), lambda t, j, te, nt: (te[t], 0, fcol(t, j, te, nt))),
                  pl.BlockSpec((1, d, tf), lambda t, j, te, nt: (te[t], 0, fcol(t, j, te, nt))),
                  pl.BlockSpec((1, tf, d), lambda t, j, te, nt: (te[t], fcol(t, j, te, nt), 0))],
        out_specs=pl.BlockSpec((tm, d), lambda t, j, te, nt: (t, 0)),
        scratch_shapes=[pltpu.VMEM((tm, d), F32)])
    return pl.pallas_call(
        _ffn_kernel, grid_spec=grid_spec,
        out_shape=jax.ShapeDtypeStruct((rows, d), BF16),
        compiler_params=_params(2),
        name="swiglu_ffn",
    )(tile_expert, n_live, x, wg, wu, wd)


def _router_kernel(x_ref, w_ref, b_ref, idx_ref, gate_ref):
    lg = jnp.dot(x_ref[...], w_ref[...], preferred_element_type=F32) + b_ref[...]
    lane = lax.broadcasted_iota(I32, lg.shape, 1)
    v1 = jnp.max(lg, axis=1, keepdims=True)
    i1 = jnp.min(jnp.where(lg == v1, lane, LANES), axis=1, keepdims=True)
    lg2 = jnp.where(lane == i1, -jnp.inf, lg)
    v2 = jnp.max(lg2, axis=1, keepdims=True)
    i2 = jnp.min(jnp.where(lg2 == v2, lane, LANES), axis=1, keepdims=True)
    e2 = jnp.exp(v2 - v1)
    denom = 1.0 + e2
    idx_ref[...] = jnp.where(lane == 0, i1, jnp.where(lane == 1, i2, 0))
    gate_ref[...] = jnp.where(lane == 0, 1.0 / denom, jnp.where(lane == 1, e2 / denom, 0.0))


def _router(x1, router, router_b, tm):
    m, d = x1.shape
    w = jnp.zeros((d, LANES), F32).at[:, :N_EXPERTS].set(router.astype(F32))
    b = jnp.full((1, LANES), -jnp.inf, F32).at[0, :N_EXPERTS].set(router_b.astype(F32))
    return pl.pallas_call(
        _router_kernel,
        grid=(m // tm,),
        in_specs=[pl.BlockSpec((tm, d), lambda i: (i, 0)),
                  pl.BlockSpec((d, LANES), lambda i: (0, 0)),
                  pl.BlockSpec((1, LANES), lambda i: (0, 0))],
        out_specs=[pl.BlockSpec((tm, LANES), lambda i: (i, 0)),
                   pl.BlockSpec((tm, LANES), lambda i: (i, 0))],
        out_shape=[jax.ShapeDtypeStruct((m, LANES), I32), jax.ShapeDtypeStruct((m, LANES), F32)],
        compiler_params=_params(1),
        name="router_top2",
    )(x1, w, b)


def _dispatch_plan(top_i, tm):
    m = top_i.shape[0]
    n_assign = m * TOP_K
    e_flat = top_i.reshape(n_assign)
    onehot = (e_flat[:, None] == jnp.arange(N_EXPERTS, dtype=I32)[None, :]).astype(I32)
    csum = jnp.cumsum(onehot, axis=0)
    counts = csum[-1]
    rank = jnp.sum(csum * onehot, axis=1) - 1
    padded = ((counts + tm - 1) // tm) * tm
    ends = jnp.cumsum(padded)
    pos = (ends - padded)[e_flat] + rank
    n_rows = n_assign + N_EXPERTS * tm
    src = jnp.zeros((n_rows,), I32).at[pos].set(jnp.arange(n_assign, dtype=I32) // TOP_K)
    tile_start = jnp.arange(n_rows // tm, dtype=I32) * tm
    tile_expert = jnp.minimum(jnp.searchsorted(ends, tile_start, side="right"),
                              N_EXPERTS - 1).astype(I32)
    n_live = (ends[-1] // tm).astype(I32).reshape(1)
    return src, pos.reshape(m, TOP_K), tile_expert, n_live


def _pick(n, candidates):
    for c in candidates:
        if n % c == 0:
            return c
    return n


def _forward(x, p, w_in, conv_w, conv_b, lru_wa, lru_ba, lru_wx, lru_bx, lru_lam, w_br_attn,
             w_br_rnn, w_o, rel_bias, ln1_g, ln1_b, ffn_w_gate, ffn_w_up, ffn_w_down, moe_router,
             moe_router_b, moe_w_gate, moe_w_up, moe_w_down, ple_w_gate, ple_w_proj, ln2_g, ln2_b):
    batch, seq, d = x.shape
    depth = w_in.shape[0]
    m = batch * seq
    n_heads = d // HEAD_DIM
    alpha = (2 * depth) ** 0.25
    d_q, d_qi = n_heads * HEAD_DIM, IDX_HEADS * IDX_DIM
    assert d_q == d and w_br_rnn.shape[1] == d and (d_q + 4 * d) % d_qi == 0

    sizes = (d_q, KV_DIM, KV_DIM, d_qi, IDX_DIM, IDX_HEADS, d, d, d, d)
    offs = np.concatenate([[0], np.cumsum(sizes)])
    col = lambda w, k: w[:, offs[k]:offs[k + 1]]
    q_scale = HEAD_DIM ** -0.5 * LOG2E
    big_scale = jnp.concatenate([jnp.full((1, d_q), q_scale, F32),
                                 jnp.ones((1, 4 * d + d_qi), F32)], axis=1)
    small_scale = jnp.ones((1, 2 * LANES), F32)
    xr_block, yr_block, ga_block, gr_block = 1, 2, 3, 4
    qi_block = (d_q + 4 * d) // d_qi

    near_bias, far_bias = _bias_tables(rel_bias)

    tm = _pick(m, (1024, 512, 256, 128))
    tn_big = _pick(d_q + 4 * d + d_qi, (512, 256, 128))
    ts = _pick(seq, (256, 128))
    tm_res = _pick(m, (512, 256, 128))

    xf = x.reshape(m, d)
    xb = xf.astype(BF16)
    p_all = p.reshape(depth * m, p.shape[-1])

    for i in range(depth):
        w = w_in[i]
        w_big = jnp.concatenate([col(w, 0), col(w, 6), col(w, 7), col(w, 8), col(w, 9), col(w, 3)],
                                axis=1).astype(BF16)
        w_small = jnp.concatenate(
            [col(w, 1), col(w, 2), col(w, 4), col(w, 5),
             jnp.zeros((d, 2 * LANES - 2 * KV_DIM - IDX_DIM - IDX_HEADS), w.dtype)], axis=1).astype(BF16)
        zb = _matmul(xb, w_big, big_scale, BF16, tm, tn_big)
        zs = _matmul(xb, w_small, small_scale, F32, tm, 2 * LANES)

        o_attn = _attention(zs, zb, near_bias, far_bias, batch, seq, n_heads, qi_block)
        o_rnn = _rglru(zb, conv_w[i], conv_b[i], lru_wa[i], lru_ba[i], lru_wx[i], lru_bx[i],
                       lru_lam[i], batch, seq, d, xr_block, yr_block, ts)
        x1, x1b = _merge_ln(o_attn, o_rnn, zb, xf, w_br_attn[i].astype(BF16),
                            w_br_rnn[i].astype(BF16), w_o[i].astype(BF16), ln1_g[i], ln1_b[i],
                            alpha, ga_block, gr_block, tm_res)

        j = i // 2
        if i % 2 == 0:
            f_dim = ffn_w_gate.shape[-1]
            tf = _pick(f_dim, (512, f_dim // 2))
            n_tiles = m // tm
            f_parts = [_ffn(x1b, jnp.zeros((n_tiles,), I32), jnp.full((1,), n_tiles, I32),
                            ffn_w_gate[j][None].astype(BF16), ffn_w_up[j][None].astype(BF16),
                            ffn_w_down[j][None].astype(BF16), tm, tf)]
        else:
            f_dim = moe_w_gate.shape[-1]
            tf = _pick(f_dim, (512, f_dim // 2))
            idx, gates = _router(x1, moe_router[j], moe_router_b[j], tm_res)
            src, pos, tile_expert, n_live = _dispatch_plan(idx[:, :TOP_K], tm)
            y = _ffn(x1b[src], tile_expert, n_live, moe_w_gate[j].astype(BF16),
                     moe_w_up[j].astype(BF16), moe_w_down[j].astype(BF16), tm, tf)
            f_parts = [y[pos[:, 0]], y[pos[:, 1]], gates]

        xf, xb = _ple_ln(x1, x1b, p_all, i, ple_w_gate[i].astype(BF16), ple_w_proj[i].astype(BF16),
                         ln2_g[i], ln2_b[i], f_parts, alpha, tm_res)

    return xf.reshape(batch, seq, d)


@jax.jit
def kernel(x, p, w_in, conv_w, conv_b, lru_wa, lru_ba, lru_wx, lru_bx, lru_lam, w_br_attn, w_br_rnn,
           w_o, rel_bias, ln1_g, ln1_b, ffn_w_gate, ffn_w_up, ffn_w_down, moe_router, moe_router_b,
           moe_w_gate, moe_w_up, moe_w_down, ple_w_gate, ple_w_proj, ln2_g, ln2_b):
    return _forward(x, p, w_in, conv_w, conv_b, lru_wa, lru_ba, lru_wx, lru_bx, lru_lam, w_br_attn,
                    w_br_rnn, w_o, rel_bias, ln1_g, ln1_b, ffn_w_gate, ffn_w_up, ffn_w_down,
                    moe_router, moe_router_b, moe_w_gate, moe_w_up, moe_w_down, ple_w_gate,
                    ple_w_proj, ln2_g, ln2_b)
```

```python
import functools
import math

import jax
import jax.numpy as jnp
import numpy as np
from jax import lax
from jax.experimental import pallas as pl
from jax.experimental.pallas import tpu as pltpu

CHUNK = 64
Q_BLOCK = 128
HEAD_DIM = 64
KV_DIM = 64
IDX_HEADS = 8
IDX_DIM = 64
INDEX_TOPK = 256
RNN_BLOCKS = 8
CONV_W = 4
LRU_C = 8.0
REL_BUCKETS = 32
REL_MAX_DIST = 128
N_EXPERTS = 8
TOP_K = 2
LN_EPS = 1e-5

LANES = 128
SUBLANES = 8
VMEM_LIMIT_BYTES = 56 * 1024 * 1024

KEY_TILE = 2 * Q_BLOCK
COUNT_TILE = 2 * KEY_TILE
HEADS_PER_DOT = 2
N_BIAS_LANES = 3
INT_MIN = -(2 ** 31)
NEG_BIG = -1e30
LOG2E = math.log2(math.e)

BF16 = jnp.bfloat16
F32 = jnp.float32
I32 = jnp.int32

_NT_DIMS = (((1,), (1,)), ((), ()))


def _params(n_axes):
    return pltpu.CompilerParams(
        dimension_semantics=("arbitrary",) * n_axes,
        vmem_limit_bytes=VMEM_LIMIT_BYTES)


def _mm_kernel(x_ref, w_ref, s_ref, o_ref):
    acc = jnp.dot(x_ref[...], w_ref[...], preferred_element_type=F32)
    o_ref[...] = (acc * s_ref[...]).astype(o_ref.dtype)


def _matmul(x, w, scale, out_dtype, tm, tn):
    m, k = x.shape
    n = w.shape[1]
    return pl.pallas_call(
        _mm_kernel,
        grid=(m // tm, n // tn),
        in_specs=[pl.BlockSpec((tm, k), lambda i, j: (i, 0)),
                  pl.BlockSpec((k, tn), lambda i, j: (0, j)),
                  pl.BlockSpec((1, tn), lambda i, j: (0, j))],
        out_specs=pl.BlockSpec((tm, tn), lambda i, j: (i, j)),
        out_shape=jax.ShapeDtypeStruct((m, n), out_dtype),
        compiler_params=_params(2),
        name="proj_in",
    )(x, w, scale)


def _fold8(x, op):
    parts = [x[r * SUBLANES:(r + 1) * SUBLANES, :] for r in range(x.shape[0] // SUBLANES)]
    while len(parts) > 1:
        nxt = [op(parts[j], parts[j + 1]) for j in range(0, len(parts) - 1, 2)]
        if len(parts) % 2:
            nxt.append(parts[-1])
        parts = nxt
    return parts[0]


def _attn_kernel(zs_ref, q_ref, qi_ref, fbq_ref, nb_ref, o_ref,
                 kp_s, ki_s, vt_s, q_s, qi_s, skey_s, mb_s, l_s, acc_s, ot_s,
                 *, k_top, n_heads, pos_bits):
    b = pl.program_id(0)
    i = pl.program_id(1)
    seq = zs_ref.shape[0]
    n_dots = n_heads // HEADS_PER_DOT
    dot_w = HEADS_PER_DOT * Q_BLOCK
    start = pl.multiple_of(i * Q_BLOCK, Q_BLOCK)
    n_att = (i + 2) // 2
    n_cnt = (n_att + 1) // 2

    @pl.when((b == 0) & (i == 0))
    def _():
        for h in range(n_heads):
            q_s[h * Q_BLOCK:(h + 1) * Q_BLOCK, HEAD_DIM:] = jnp.broadcast_to(
                fbq_ref[h:h + 1, :], (Q_BLOCK, LANES - HEAD_DIM)).astype(BF16)

    @pl.when(i == 0)
    def _():
        kv = zs_ref[:, 0:2 * KV_DIM]
        col = lax.broadcasted_iota(I32, kv.shape, 1)
        ones = jnp.where(col < KV_DIM + N_BIAS_LANES, 1.0, 0.0)
        kp_s[...] = jnp.where(col < KV_DIM, kv, ones).astype(BF16)
        vt_s[...] = kv.T[KV_DIM:2 * KV_DIM, :].astype(BF16)
        ki_s[...] = zs_ref[:, 2 * KV_DIM:2 * KV_DIM + IDX_DIM].astype(BF16)
        skey_s[...] = jnp.full(skey_s.shape, INT_MIN, I32)

    for h in range(n_heads):
        q_s[h * Q_BLOCK:(h + 1) * Q_BLOCK, 0:HEAD_DIM] = q_ref[:, h * HEAD_DIM:(h + 1) * HEAD_DIM]
    for h in range(IDX_HEADS):
        qi_s[h * Q_BLOCK:(h + 1) * Q_BLOCK, :] = qi_ref[:, h * IDX_DIM:(h + 1) * IDX_DIM]

    w_t = zs_ref[pl.ds(start, Q_BLOCK), LANES:2 * LANES].T

    row = lax.broadcasted_iota(I32, (KEY_TILE, Q_BLOCK), 0)
    lane = lax.broadcasted_iota(I32, (KEY_TILE, Q_BLOCK), 1)
    row_cnt = lax.broadcasted_iota(I32, (COUNT_TILE, Q_BLOCK), 0)
    q_chunk = (start + lane) // CHUNK

    def score_tile(g, carry):
        off = pl.multiple_of(g * KEY_TILE, KEY_TILE)
        ki_t = ki_s[pl.ds(off, KEY_TILE), :]
        acc = jnp.zeros((KEY_TILE, Q_BLOCK), F32)
        for c in range(IDX_HEADS // HEADS_PER_DOT):
            d = lax.dot_general(ki_t, qi_s[c * dot_w:(c + 1) * dot_w, :], _NT_DIMS,
                                preferred_element_type=F32)
            for s in range(HEADS_PER_DOT):
                h = c * HEADS_PER_DOT + s
                acc = acc + (jnp.maximum(d[:, s * Q_BLOCK:(s + 1) * Q_BLOCK], 0.0)
                             * w_t[IDX_DIM + h:IDX_DIM + h + 1, :])
        bits = lax.bitcast_convert_type(acc, I32)
        key = bits ^ ((bits >> 31) & jnp.int32(0x7FFFFFFF))
        k_chunk = (off + row) // CHUNK
        skey_s[pl.ds(off, KEY_TILE), :] = jnp.where(k_chunk <= q_chunk, key, INT_MIN)
        return carry

    lax.fori_loop(0, n_att, score_tile, 0)

    def count(pred):
        def body(g, c):
            off = pl.multiple_of(g * COUNT_TILE, COUNT_TILE)
            hit = pred(skey_s[pl.ds(off, COUNT_TILE), :], off + row_cnt)
            return c + _fold8(jnp.where(hit, 1, 0).astype(I32), jnp.add)
        c = lax.fori_loop(0, n_cnt, body, jnp.zeros((SUBLANES, Q_BLOCK), I32))
        return jnp.sum(c, axis=0, keepdims=True)

    def bisect(it, t):
        cand = t ^ lax.shift_left(jnp.int32(1), 31 - it)
        c = count(lambda key, pos: key >= cand)
        return jnp.where(c >= k_top, cand, t)

    t = lax.fori_loop(0, 32, bisect, jnp.full((1, Q_BLOCK), INT_MIN, I32))

    n_ge = count(lambda key, pos: key >= t)

    @pl.when(jnp.max(n_ge) > k_top)
    def _():
        n_gt = count(lambda key, pos: key > t)
        r_m1 = k_top - n_gt - 1

        def pos_search(it, j):
            cand = j | lax.shift_left(jnp.int32(1), pos_bits - 1 - it)
            c = count(lambda key, pos: (key == t) & (pos < cand))
            return jnp.where(c <= r_m1, cand, j)

        j_cut = lax.fori_loop(0, pos_bits, pos_search, jnp.zeros((1, Q_BLOCK), I32))

        def drop(g, carry):
            off = pl.multiple_of(g * KEY_TILE, KEY_TILE)
            key = skey_s[pl.ds(off, KEY_TILE), :]
            cut = (key == t) & ((off + row) > j_cut)
            skey_s[pl.ds(off, KEY_TILE), :] = jnp.where(cut, INT_MIN, key)
            return carry

        lax.fori_loop(0, n_att, drop, 0)

    t_sel = jnp.maximum(t, INT_MIN + 1)

    def mask_tile(g, carry):
        off = pl.multiple_of(g * KEY_TILE, KEY_TILE)
        mb_s[pl.ds(off, KEY_TILE), :] = jnp.where(
            skey_s[pl.ds(off, KEY_TILE), :] >= t_sel, 0.0, NEG_BIG)
        return carry

    lax.fori_loop(0, n_att, mask_tile, 0)

    def logits_tile(g, m8, table):
        off = pl.multiple_of(g * KEY_TILE, KEY_TILE)
        kp_t = kp_s[pl.ds(off, KEY_TILE), :]
        mb = mb_s[pl.ds(off, KEY_TILE), :]
        out = []
        for c in range(n_dots):
            lg = lax.dot_general(kp_t, q_s[c * dot_w:(c + 1) * dot_w, :], _NT_DIMS,
                                 preferred_element_type=F32)
            for s in range(HEADS_PER_DOT):
                h = c * HEADS_PER_DOT + s
                x = lg[:, s * Q_BLOCK:(s + 1) * Q_BLOCK] + mb
                if table is not None:
                    x = x + nb_ref[table, h]
                l_s[h, pl.ds(off, KEY_TILE), :] = x
                out.append(jnp.maximum(m8[h], _fold8(x, jnp.maximum)))
        return tuple(out)

    n_near = jnp.where((i & 1) == 0, jnp.minimum(n_att, 2), 1)
    n_far = n_att - n_near
    m8 = tuple(jnp.full((SUBLANES, Q_BLOCK), NEG_BIG, F32) for _ in range(n_heads))
    m8 = lax.fori_loop(0, n_far, lambda g, m: logits_tile(g, m, None), m8)
    m8 = lax.fori_loop(n_far, n_att, lambda g, m: logits_tile(g, m, 2 * g - i + 2), m8)
    m_row = [jnp.max(m, axis=0, keepdims=True) for m in m8]

    acc_s[...] = jnp.zeros(acc_s.shape, F32)

    def pv_tile(g, l8):
        off = pl.multiple_of(g * KEY_TILE, KEY_TILE)
        vt_t = vt_s[:, pl.ds(off, KEY_TILE)]
        out = []
        for c in range(n_dots):
            ps = []
            for s in range(HEADS_PER_DOT):
                h = c * HEADS_PER_DOT + s
                p = jnp.exp2(l_s[h, pl.ds(off, KEY_TILE), :] - m_row[h])
                out.append(l8[h] + _fold8(p, jnp.add))
                ps.append(p.astype(BF16))
            acc_s[c] += jnp.dot(vt_t, jnp.concatenate(ps, axis=1), preferred_element_type=F32)
        return tuple(out)

    l8 = tuple(jnp.zeros((SUBLANES, Q_BLOCK), F32) for _ in range(n_heads))
    l8 = lax.fori_loop(0, n_att, pv_tile, l8)

    for c in range(n_dots):
        acc = acc_s[c]
        for s in range(HEADS_PER_DOT):
            h = c * HEADS_PER_DOT + s
            denom = jnp.sum(l8[h], axis=0, keepdims=True)
            ot_s[h * HEAD_DIM:(h + 1) * HEAD_DIM, :] = acc[:, s * Q_BLOCK:(s + 1) * Q_BLOCK] / denom
    o_ref[...] = ot_s[...].T.astype(o_ref.dtype)


def _attention(zs, zb, near_bias, far_q, batch, seq, n_heads, qi_block):
    d_q = n_heads * HEAD_DIM
    nqb = seq // Q_BLOCK
    k_top = min(INDEX_TOPK, seq // 4)
    assert seq % COUNT_TILE == 0 and n_heads % HEADS_PER_DOT == 0
    kern = functools.partial(_attn_kernel, k_top=k_top, n_heads=n_heads,
                             pos_bits=max(1, (seq - 1).bit_length()))
    return pl.pallas_call(
        kern,
        grid=(batch, nqb),
        in_specs=[
            pl.BlockSpec((seq, 2 * LANES), lambda b, i: (b, 0)),
            pl.BlockSpec((Q_BLOCK, d_q), lambda b, i: (b * nqb + i, 0)),
            pl.BlockSpec((Q_BLOCK, IDX_HEADS * IDX_DIM), lambda b, i: (b * nqb + i, qi_block)),
            pl.BlockSpec(far_q.shape, lambda b, i: (0, 0)),
            pl.BlockSpec(near_bias.shape, lambda b, i: (0, 0, 0, 0)),
        ],
        out_specs=pl.BlockSpec((Q_BLOCK, d_q), lambda b, i: (b * nqb + i, 0)),
        out_shape=jax.ShapeDtypeStruct((batch * seq, d_q), BF16),
        scratch_shapes=[
            pltpu.VMEM((seq, LANES), BF16),
            pltpu.VMEM((seq, IDX_DIM), BF16),
            pltpu.VMEM((KV_DIM, seq), BF16),
            pltpu.VMEM((n_heads * Q_BLOCK, LANES), BF16),
            pltpu.VMEM((IDX_HEADS * Q_BLOCK, IDX_DIM), BF16),
            pltpu.VMEM((seq, Q_BLOCK), I32),
            pltpu.VMEM((seq, Q_BLOCK), F32),
            pltpu.VMEM((n_heads, seq, Q_BLOCK), F32),
            pltpu.VMEM((n_heads // HEADS_PER_DOT, KV_DIM, HEADS_PER_DOT * Q_BLOCK), F32),
            pltpu.VMEM((d_q, Q_BLOCK), F32),
        ],
        compiler_params=_params(2),
        name="dsa_attention",
    )(zs, zb, zb, far_q, near_bias)


def _t5_bucket(rel):
    half = REL_BUCKETS // 2
    max_exact = half // 2
    ret = jnp.where(rel > 0, half, 0)
    n = jnp.abs(rel)
    nf = jnp.maximum(n, 1).astype(F32)
    large = max_exact + (jnp.log(nf / max_exact) / math.log(REL_MAX_DIST / max_exact)
                         * (half - max_exact)).astype(I32)
    large = jnp.minimum(large, half - 1)
    return ret + jnp.where(n < max_exact, n, large)


def _bias_tables(rel_bias):
    n_heads = rel_bias.shape[1]
    far = rel_bias[_t5_bucket(jnp.asarray(-2 * REL_MAX_DIST, I32))].astype(F32) * LOG2E
    terms, rest = [], far
    for _ in range(N_BIAS_LANES):
        terms.append(rest.astype(BF16).astype(F32))
        rest = rest - terms[-1]
    far_q = jnp.zeros((n_heads, LANES - HEAD_DIM), F32).at[:, :N_BIAS_LANES].set(
        jnp.stack(terms, axis=1))
    far_sum = sum(terms[1:], terms[0])
    delta = (np.arange(3)[:, None, None] - 2) * Q_BLOCK
    rel = delta + np.arange(KEY_TILE)[None, :, None] - np.arange(Q_BLOCK)[None, None, :]
    near = rel_bias[_t5_bucket(jnp.asarray(rel, I32))].astype(F32) * LOG2E
    near = jnp.moveaxis(near, -1, 1) - far_sum[None, :, None, None]
    return near, far_q


def _neg_expm1(y):
    series = -y * (1.0 + y * (1.0 / 2 + y * (1.0 / 6 + y * (1.0 / 24 + y * (1.0 / 120)))))
    return jnp.where(y > -0.1, series, 1.0 - jnp.exp(y))


def _rglru_kernel(xr_ref, yr_ref, cw_ref, cb_ref, wa_ref, ba_ref, wx_ref, bx_ref, lam_ref,
                  o_ref, ext_s, a_s, u_s, h_s, *, ts, bw):
    j = pl.program_id(1)

    @pl.when(j == 0)
    def _():
        ext_s[0:SUBLANES, :] = jnp.zeros((SUBLANES, ext_s.shape[1]), F32)
        h_s[...] = jnp.zeros(h_s.shape, F32)

    x = xr_ref[...].astype(F32)
    ext_s[SUBLANES:SUBLANES + ts, :] = x
    xc = ext_s[SUBLANES - 3:SUBLANES - 3 + ts, :] * cw_ref[0:1, :]
    xc = xc + ext_s[SUBLANES - 2:SUBLANES - 2 + ts, :] * cw_ref[1:2, :]
    xc = xc + ext_s[SUBLANES - 1:SUBLANES - 1 + ts, :] * cw_ref[2:3, :]
    xc = xc + x * cw_ref[3:4, :]
    xc = xc + cb_ref[...]
    ext_s[0:SUBLANES, :] = ext_s[ts:ts + SUBLANES, :]

    sp = jax.nn.softplus(-lam_ref[...])
    for n in range(RNN_BLOCKS):
        cs = slice(n * bw, (n + 1) * bw)
        xb = xc[:, cs]
        xb16 = xb.astype(BF16)
        r = jax.nn.sigmoid(jnp.dot(xb16, wa_ref[n], preferred_element_type=F32) + ba_ref[:, cs])
        g = jax.nn.sigmoid(jnp.dot(xb16, wx_ref[n], preferred_element_type=F32) + bx_ref[:, cs])
        log_a = -LRU_C * r * sp[:, cs]
        a_s[:, cs] = jnp.exp(log_a)
        u_s[:, cs] = jnp.sqrt(_neg_expm1(2.0 * log_a)) * (g * xb)

    def step(t, h):
        h = a_s[pl.ds(t, 1), :] * h + u_s[pl.ds(t, 1), :]
        u_s[pl.ds(t, 1), :] = h
        return h

    h_s[...] = lax.fori_loop(0, ts, step, h_s[...], unroll=8)
    o_ref[...] = (u_s[...] * jax.nn.gelu(yr_ref[...].astype(F32))).astype(o_ref.dtype)


def _rglru(zb, conv_w, conv_b, wa, ba, wx, bx, lam, batch, seq, width, xr_block, yr_block, ts):
    nts = seq // ts
    bw = width // RNN_BLOCKS
    row = lambda v: v.reshape(1, width)
    full = lambda shape: pl.BlockSpec(shape, lambda b, j: (0,) * len(shape))
    return pl.pallas_call(
        functools.partial(_rglru_kernel, ts=ts, bw=bw),
        grid=(batch, nts),
        in_specs=[pl.BlockSpec((ts, width), lambda b, j: (b * nts + j, xr_block)),
                  pl.BlockSpec((ts, width), lambda b, j: (b * nts + j, yr_block)),
                  full((CONV_W, width)), full((1, width)),
                  full((RNN_BLOCKS, bw, bw)), full((1, width)),
                  full((RNN_BLOCKS, bw, bw)), full((1, width)), full((1, width))],
        out_specs=pl.BlockSpec((ts, width), lambda b, j: (b * nts + j, 0)),
        out_shape=jax.ShapeDtypeStruct((batch * seq, width), BF16),
        scratch_shapes=[pltpu.VMEM((ts + 2 * SUBLANES, width), F32),
                        pltpu.VMEM((ts, width), F32),
                        pltpu.VMEM((ts, width), F32),
                        pltpu.VMEM((1, width), F32)],
        compiler_params=_params(2),
        name="rglru",
    )(zb, zb, conv_w, row(conv_b), wa.astype(BF16), row(ba), wx.astype(BF16), row(bx), row(lam))


def _layer_norm(y, g, b):
    mu = jnp.mean(y, axis=-1, keepdims=True)
    var = jnp.mean(jnp.square(y - mu), axis=-1, keepdims=True)
    return (y - mu) * lax.rsqrt(var + LN_EPS) * g + b


def _merge_kernel(oa_ref, or_ref, ga_ref, gr_ref, x_ref, wba_ref, wbr_ref, wo_ref, g_ref, b_ref,
                  x1_ref, x1b_ref, *, alpha):
    a = jnp.dot(oa_ref[...], wba_ref[...], preferred_element_type=F32)
    r = jnp.dot(or_ref[...], wbr_ref[...], preferred_element_type=F32)
    merged = (jax.nn.sigmoid(ga_ref[...].astype(F32)) * a
              + jax.nn.sigmoid(gr_ref[...].astype(F32)) * r)
    mix = jnp.dot(merged.astype(BF16), wo_ref[...], preferred_element_type=F32)
    x1 = _layer_norm(alpha * x_ref[...] + mix, g_ref[...], b_ref[...])
    x1_ref[...] = x1
    x1b_ref[...] = x1.astype(BF16)


def _merge_ln(oa, orn, zb, x, wba, wbr, wo, g, b, alpha, ga_block, gr_block, tm):
    m, d = x.shape
    tok = lambda blk: pl.BlockSpec((tm, d), lambda i: (i, blk))
    full = lambda shape: pl.BlockSpec(shape, lambda i: (0,) * len(shape))
    return pl.pallas_call(
        functools.partial(_merge_kernel, alpha=alpha),
        grid=(m // tm,),
        in_specs=[tok(0), tok(0), tok(ga_block), tok(gr_block), tok(0),
                  full((d, d)), full((d, d)), full((d, d)), full((1, d)), full((1, d))],
        out_specs=[tok(0), tok(0)],
        out_shape=[jax.ShapeDtypeStruct((m, d), F32), jax.ShapeDtypeStruct((m, d), BF16)],
        compiler_params=_params(1),
        name="merge_ln1",
    )(oa, orn, zb, zb, x, wba, wbr, wo, g.reshape(1, d), b.reshape(1, d))


def _ple_kernel(*refs, alpha, routed):
    if routed:
        (x1_ref, x1b_ref, p_ref, pg_ref, pp_ref, g_ref, b_ref,
         ya_ref, yb_ref, gt_ref, x2_ref, x2b_ref) = refs
        f = (gt_ref[:, 0:1] * ya_ref[...].astype(F32)
             + gt_ref[:, 1:2] * yb_ref[...].astype(F32))
    else:
        x1_ref, x1b_ref, p_ref, pg_ref, pp_ref, g_ref, b_ref, f_ref, x2_ref, x2b_ref = refs
        f = f_ref[...].astype(F32)
    gate = jax.nn.sigmoid(jnp.dot(x1b_ref[...], pg_ref[...], preferred_element_type=F32))
    proj = jnp.dot(p_ref[...].astype(BF16), pp_ref[...], preferred_element_type=F32)
    x2 = _layer_norm(alpha * x1_ref[...] + f + gate * proj, g_ref[...], b_ref[...])
    x2_ref[...] = x2
    x2b_ref[...] = x2.astype(BF16)


def _ple_ln(x1, x1b, p_all, layer, pg, pp, g, b, f_parts, alpha, tm):
    m, d = x1.shape
    pd = p_all.shape[1]
    nt = m // tm
    tok = pl.BlockSpec((tm, d), lambda i: (i, 0))
    full = lambda shape: pl.BlockSpec(shape, lambda i: (0,) * len(shape))
    routed = len(f_parts) == 3
    f_specs = [tok, tok, pl.BlockSpec((tm, LANES), lambda i: (i, 0))] if routed else [tok]
    return pl.pallas_call(
        functools.partial(_ple_kernel, alpha=alpha, routed=routed),
        grid=(nt,),
        in_specs=[tok, tok, pl.BlockSpec((tm, pd), lambda i: (layer * nt + i, 0)),
                  full((d, d)), full((pd, d)), full((1, d)), full((1, d))] + f_specs,
        out_specs=[tok, tok],
        out_shape=[jax.ShapeDtypeStruct((m, d), F32), jax.ShapeDtypeStruct((m, d), BF16)],
        compiler_params=_params(1),
        name="ple_ln2",
    )(x1, x1b, p_all, pg, pp, g.reshape(1, d), b.reshape(1, d), *f_parts)


def _ffn_kernel(te_ref, nt_ref, x_ref, wg_ref, wu_ref, wd_ref, o_ref, acc_s):
    t = pl.program_id(0)
    j = pl.program_id(1)
    last = pl.num_programs(1) - 1
    live = t < nt_ref[0]

    @pl.when(live)
    def _():
        x = x_ref[...]
        gate = jnp.dot(x, wg_ref[0], preferred_element_type=F32)
        up = jnp.dot(x, wu_ref[0], preferred_element_type=F32)
        hid = (jax.nn.silu(gate) * up).astype(BF16)
        part = jnp.dot(hid, wd_ref[0], preferred_element_type=F32)

        @pl.when(j == 0)
        def _():
            acc_s[...] = part

        @pl.when(j > 0)
        def _():
            acc_s[...] += part

    @pl.when(j == last)
    def _():
        o_ref[...] = jnp.where(live, acc_s[...], 0.0).astype(o_ref.dtype)


def _ffn(x, tile_expert, n_live, wg, wu, wd, tm, tf):
    rows, d = x.shape
    f = wg.shape[2]
    fcol = lambda t, j, te, nt: jnp.where(t < nt[0], j, 0)
    grid_spec = pltpu.PrefetchScalarGridSpec(
        num_scalar_prefetch=2,
        grid=(rows // tm, f // tf),
        in_specs=[pl.BlockSpec((tm, d), lambda t, j, te, nt: (t, 0)),
                  pl.BlockSpec((1, d, tf), lambda t, j, te, nt: (te[t], 0, fcol(t, j, te, nt))),
                  pl.BlockSpec((1, d, tf), lambda t, j, te, nt: (te[t], 0, fcol(t, j, te, nt))),
                  pl.BlockSpec((1, tf, d), lambda t, j, te, nt: (te[t], fcol(t, j, te, nt), 0))],
        out_specs=pl.BlockSpec((tm, d), lambda t, j, te, nt: (t, 0)),
        scratch_shapes=[pltpu.VMEM((tm, d), F32)])
    return pl.pallas_call(
        _ffn_kernel, grid_spec=grid_spec,
        out_shape=jax.ShapeDtypeStruct((rows, d), BF16),
        compiler_params=_params(2),
        name="swiglu_ffn",
    )(tile_expert, n_live, x, wg, wu, wd)


def _router_kernel(x_ref, w_ref, b_ref, idx_ref, gate_ref):
    lg = jnp.dot(x_ref[...], w_ref[...], preferred_element_type=F32) + b_ref[...]
    lane = lax.broadcasted_iota(I32, lg.shape, 1)
    v1 = jnp.max(lg, axis=1, keepdims=True)
    i1 = jnp.min(jnp.where(lg == v1, lane, LANES), axis=1, keepdims=True)
    lg2 = jnp.where(lane == i1, -jnp.inf, lg)
    v2 = jnp.max(lg2, axis=1, keepdims=True)
    i2 = jnp.min(jnp.where(lg2 == v2, lane, LANES), axis=1, keepdims=True)
    e2 = jnp.exp(v2 - v1)
    denom = 1.0 + e2
    idx_ref[...] = jnp.where(lane == 0, i1, jnp.where(lane == 1, i2, 0))
    gate_ref[...] = jnp.where(lane == 0, 1.0 / denom, jnp.where(lane == 1, e2 / denom, 0.0))


def _router(x1, router, router_b, tm):
    m, d = x1.shape
    w = jnp.zeros((d, LANES), F32).at[:, :N_EXPERTS].set(router.astype(F32))
    b = jnp.full((1, LANES), -jnp.inf, F32).at[0, :N_EXPERTS].set(router_b.astype(F32))
    return pl.pallas_call(
        _router_kernel,
        grid=(m // tm,),
        in_specs=[pl.BlockSpec((tm, d), lambda i: (i, 0)),
                  pl.BlockSpec((d, LANES), lambda i: (0, 0)),
                  pl.BlockSpec((1, LANES), lambda i: (0, 0))],
        out_specs=[pl.BlockSpec((tm, LANES), lambda i: (i, 0)),
                   pl.BlockSpec((tm, LANES), lambda i: (i, 0))],
        out_shape=[jax.ShapeDtypeStruct((m, LANES), I32), jax.ShapeDtypeStruct((m, LANES), F32)],
        compiler_params=_params(1),
        name="router_top2",
    )(x1, w, b)


def _dispatch_plan(top_i, tm):
    m = top_i.shape[0]
    n_assign = m * TOP_K
    e_flat = top_i.reshape(n_assign)
    onehot = (e_flat[:, None] == jnp.arange(N_EXPERTS, dtype=I32)[None, :]).astype(I32)
    csum = jnp.cumsum(onehot, axis=0)
    counts = csum[-1]
    rank = jnp.sum(csum * onehot, axis=1) - 1
    padded = ((counts + tm - 1) // tm) * tm
    ends = jnp.cumsum(padded)
    pos = (ends - padded)[e_flat] + rank
    n_rows = n_assign + N_EXPERTS * tm
    src = jnp.zeros((n_rows,), I32).at[pos].set(jnp.arange(n_assign, dtype=I32) // TOP_K)
    tile_start = jnp.arange(n_rows // tm, dtype=I32) * tm
    tile_expert = jnp.minimum(jnp.searchsorted(ends, tile_start, side="right"),
                              N_EXPERTS - 1).astype(I32)
    n_live = (ends[-1] // tm).astype(I32).reshape(1)
    return src, pos.reshape(m, TOP_K), tile_expert, n_live


def _pick(n, candidates):
    for c in candidates:
        if n % c == 0:
            return c
    return n


def _forward(x, p, w_in, conv_w, conv_b, lru_wa, lru_ba, lru_wx, lru_bx, lru_lam, w_br_attn,
             w_br_rnn, w_o, rel_bias, ln1_g, ln1_b, ffn_w_gate, ffn_w_up, ffn_w_down, moe_router,
             moe_router_b, moe_w_gate, moe_w_up, moe_w_down, ple_w_gate, ple_w_proj, ln2_g, ln2_b):
    batch, seq, d = x.shape
    depth = w_in.shape[0]
    m = batch * seq
    n_heads = d // HEAD_DIM
    alpha = (2 * depth) ** 0.25
    d_q, d_qi = n_heads * HEAD_DIM, IDX_HEADS * IDX_DIM
    assert d_q == d and w_br_rnn.shape[1] == d and (d_q + 4 * d) % d_qi == 0

    sizes = (d_q, KV_DIM, KV_DIM, d_qi, IDX_DIM, IDX_HEADS, d, d, d, d)
    offs = np.concatenate([[0], np.cumsum(sizes)])
    col = lambda w, k: w[:, offs[k]:offs[k + 1]]
    q_scale = HEAD_DIM ** -0.5 * LOG2E
    big_scale = jnp.concatenate([jnp.full((1, d_q), q_scale, F32),
                                 jnp.ones((1, 4 * d + d_qi), F32)], axis=1)
    small_scale = jnp.ones((1, 2 * LANES), F32)
    xr_block, yr_block, ga_block, gr_block = 1, 2, 3, 4
    qi_block = (d_q + 4 * d) // d_qi

    near_bias, far_q = _bias_tables(rel_bias)

    tm = _pick(m, (1024, 512, 256, 128))
    tn_big = _pick(d_q + 4 * d + d_qi, (512, 256, 128))
    ts = _pick(seq, (256, 128))
    tm_res = _pick(m, (512, 256, 128))

    xf = x.reshape(m, d)
    xb = xf.astype(BF16)
    p_all = p.reshape(depth * m, p.shape[-1])

    for i in range(depth):
        w = w_in[i]
        w_big = jnp.concatenate([col(w, 0), col(w, 6), col(w, 7), col(w, 8), col(w, 9), col(w, 3)],
                                axis=1).astype(BF16)
        w_small = jnp.concatenate(
            [col(w, 1), col(w, 2), col(w, 4), col(w, 5),
             jnp.zeros((d, 2 * LANES - 2 * KV_DIM - IDX_DIM - IDX_HEADS), w.dtype)], axis=1).astype(BF16)
        zb = _matmul(xb, w_big, big_scale, BF16, tm, tn_big)
        zs = _matmul(xb, w_small, small_scale, F32, tm, 2 * LANES)

        o_attn = _attention(zs, zb, near_bias, far_q, batch, seq, n_heads, qi_block)
        o_rnn = _rglru(zb, conv_w[i], conv_b[i], lru_wa[i], lru_ba[i], lru_wx[i], lru_bx[i],
                       lru_lam[i], batch, seq, d, xr_block, yr_block, ts)
        x1, x1b = _merge_ln(o_attn, o_rnn, zb, xf, w_br_attn[i].astype(BF16),
                            w_br_rnn[i].astype(BF16), w_o[i].astype(BF16), ln1_g[i], ln1_b[i],
                            alpha, ga_block, gr_block, tm_res)

        j = i // 2
        if i % 2 == 0:
            f_dim = ffn_w_gate.shape[-1]
            tf = _pick(f_dim, (512, f_dim // 2))
            n_tiles = m // tm
            f_parts = [_ffn(x1b, jnp.zeros((n_tiles,), I32), jnp.full((1,), n_tiles, I32),
                            ffn_w_gate[j][None].astype(BF16), ffn_w_up[j][None].astype(BF16),
                            ffn_w_down[j][None].astype(BF16), tm, tf)]
        else:
            f_dim = moe_w_gate.shape[-1]
            tf = _pick(f_dim, (512, f_dim // 2))
            idx, gates = _router(x1, moe_router[j], moe_router_b[j], tm_res)
            src, pos, tile_expert, n_live = _dispatch_plan(idx[:, :TOP_K], tm)
            y = _ffn(x1b[src], tile_expert, n_live, moe_w_gate[j].astype(BF16),
                     moe_w_up[j].astype(BF16), moe_w_down[j].astype(BF16), tm, tf)
            f_parts = [y[pos[:, 0]], y[pos[:, 1]], gates]

        xf, xb = _ple_ln(x1, x1b, p_all, i, ple_w_gate[i].astype(BF16), ple_w_proj[i].astype(BF16),
                         ln2_g[i], ln2_b[i], f_parts, alpha, tm_res)

    return xf.reshape(batch, seq, d)


@jax.jit
def kernel(x, p, w_in, conv_w, conv_b, lru_wa, lru_ba, lru_wx, lru_bx, lru_lam, w_br_attn, w_br_rnn,
           w_o, rel_bias, ln1_g, ln1_b, ffn_w_gate, ffn_w_up, ffn_w_down, moe_router, moe_router_b,
           moe_w_gate, moe_w_up, moe_w_down, ple_w_gate, ple_w_proj, ln2_g, ln2_b):
    return _forward(x, p, w_in, conv_w, conv_b, lru_wa, lru_ba, lru_wx, lru_bx, lru_lam, w_br_attn,
                    w_br_rnn, w_o, rel_bias, ln1_g, ln1_b, ffn_w_gate, ffn_w_up, ffn_w_down,
                    moe_router, moe_router_b, moe_w_gate, moe_w_up, moe_w_down, ple_w_gate,
                    ple_w_proj, ln2_g, ln2_b)
```

```python
import functools
import math

import jax
import jax.numpy as jnp
import numpy as np
from jax import lax
from jax.experimental import pallas as pl
from jax.experimental.pallas import tpu as pltpu

CHUNK = 64
Q_BLOCK = 128
HEAD_DIM = 64
KV_DIM = 64
IDX_HEADS = 8
IDX_DIM = 64
INDEX_TOPK = 256
RNN_BLOCKS = 8
CONV_W = 4
LRU_C = 8.0
REL_BUCKETS = 32
REL_MAX_DIST = 128
N_EXPERTS = 8
TOP_K = 2
LN_EPS = 1e-5

LANES = 128
SUBLANES = 8
VMEM_LIMIT_BYTES = 56 * 1024 * 1024

KEY_TILE = 2 * Q_BLOCK
COUNT_TILE = 2 * KEY_TILE
HEADS_PER_DOT = 2
N_BIAS_LANES = 3
QK_LEAD = 8
INT_MIN = -(2 ** 31)
I16_MIN = -(2 ** 15)
PACK16 = 2 * SUBLANES
NEG_BIG = -1e30
LOG2E = math.log2(math.e)

BF16 = jnp.bfloat16
F32 = jnp.float32
I32 = jnp.int32
I16 = jnp.int16

_NT_DIMS = (((1,), (1,)), ((), ()))


def _params(n_axes):
    return pltpu.CompilerParams(
        dimension_semantics=("arbitrary",) * n_axes,
        vmem_limit_bytes=VMEM_LIMIT_BYTES)


def _mm_kernel(x_ref, w_ref, s_ref, o_ref):
    acc = jnp.dot(x_ref[...], w_ref[...], preferred_element_type=F32)
    o_ref[...] = (acc * s_ref[...]).astype(o_ref.dtype)


def _matmul(x, w, scale, out_dtype, tm, tn):
    m, k = x.shape
    n = w.shape[1]
    return pl.pallas_call(
        _mm_kernel,
        grid=(m // tm, n // tn),
        in_specs=[pl.BlockSpec((tm, k), lambda i, j: (i, 0)),
                  pl.BlockSpec((k, tn), lambda i, j: (0, j)),
                  pl.BlockSpec((1, tn), lambda i, j: (0, j))],
        out_specs=pl.BlockSpec((tm, tn), lambda i, j: (i, j)),
        out_shape=jax.ShapeDtypeStruct((m, n), out_dtype),
        compiler_params=_params(2),
        name="proj_in",
    )(x, w, scale)


def _fold_rows(x, op, rows):
    parts = [x[r * rows:(r + 1) * rows, :] for r in range(x.shape[0] // rows)]
    while len(parts) > 1:
        nxt = [op(parts[j], parts[j + 1]) for j in range(0, len(parts) - 1, 2)]
        if len(parts) % 2:
            nxt.append(parts[-1])
        parts = nxt
    return parts[0]


def _fold8(x, op):
    return _fold_rows(x, op, SUBLANES)


def _attn_kernel(zs_ref, q_ref, qi_ref, fbq_ref, nb_ref, o_ref,
                 kp_s, ki_s, vt_s, q_s, qi_s, skey_s, hi_s, lo_s, mb_s, acc_s, ot_s,
                 *, k_top, n_heads, pos_bits):
    b = pl.program_id(0)
    i = pl.program_id(1)
    seq = zs_ref.shape[0]
    n_dots = n_heads // HEADS_PER_DOT
    dot_w = HEADS_PER_DOT * Q_BLOCK
    start = pl.multiple_of(i * Q_BLOCK, Q_BLOCK)
    n_att = (i + 2) // 2
    n_cnt = (n_att + 1) // 2

    @pl.when((b == 0) & (i == 0))
    def _():
        for h in range(n_heads):
            q_s[h * Q_BLOCK:(h + 1) * Q_BLOCK, HEAD_DIM:] = jnp.broadcast_to(
                fbq_ref[h:h + 1, :], (Q_BLOCK, LANES - HEAD_DIM)).astype(BF16)

    @pl.when(i == 0)
    def _():
        kv = zs_ref[:, 0:2 * KV_DIM]
        col = lax.broadcasted_iota(I32, kv.shape, 1)
        ones = jnp.where(col < KV_DIM + N_BIAS_LANES, 1.0, 0.0)
        kp_s[...] = jnp.where(col < KV_DIM, kv, ones).astype(BF16)
        vt_s[...] = kv.T[KV_DIM:2 * KV_DIM, :].astype(BF16)
        ki_s[...] = zs_ref[:, 2 * KV_DIM:2 * KV_DIM + IDX_DIM].astype(BF16)
        skey_s[...] = jnp.full(skey_s.shape, INT_MIN, I32)
        hi_s[...] = jnp.full(hi_s.shape, I16_MIN, I16)
        lo_s[...] = jnp.full(lo_s.shape, I16_MIN, I16)

    for h in range(n_heads):
        q_s[h * Q_BLOCK:(h + 1) * Q_BLOCK, 0:HEAD_DIM] = q_ref[:, h * HEAD_DIM:(h + 1) * HEAD_DIM]
    for h in range(IDX_HEADS):
        qi_s[h * Q_BLOCK:(h + 1) * Q_BLOCK, :] = qi_ref[:, h * IDX_DIM:(h + 1) * IDX_DIM]

    w_t = zs_ref[pl.ds(start, Q_BLOCK), LANES:2 * LANES].T

    row = lax.broadcasted_iota(I32, (KEY_TILE, Q_BLOCK), 0)
    lane = lax.broadcasted_iota(I32, (KEY_TILE, Q_BLOCK), 1)
    row_cnt = lax.broadcasted_iota(I32, (COUNT_TILE, Q_BLOCK), 0)
    q_chunk = (start + lane) // CHUNK

    def score_tile(g, carry):
        off = pl.multiple_of(g * KEY_TILE, KEY_TILE)
        ki_t = ki_s[pl.ds(off, KEY_TILE), :]
        acc = jnp.zeros((KEY_TILE, Q_BLOCK), F32)
        for c in range(IDX_HEADS // HEADS_PER_DOT):
            d = lax.dot_general(ki_t, qi_s[c * dot_w:(c + 1) * dot_w, :], _NT_DIMS,
                                preferred_element_type=F32)
            for s in range(HEADS_PER_DOT):
                h = c * HEADS_PER_DOT + s
                acc = acc + (jnp.maximum(d[:, s * Q_BLOCK:(s + 1) * Q_BLOCK], 0.0)
                             * w_t[IDX_DIM + h:IDX_DIM + h + 1, :])
        bits = lax.bitcast_convert_type(acc, I32)
        key = bits ^ ((bits >> 31) & jnp.int32(0x7FFFFFFF))
        k_chunk = (off + row) // CHUNK
        key = jnp.where(k_chunk <= q_chunk, key, INT_MIN)
        skey_s[pl.ds(off, KEY_TILE), :] = key
        hi_s[pl.ds(off, KEY_TILE), :] = (key >> 16).astype(I16)
        lo_s[pl.ds(off, KEY_TILE), :] = ((key & 0xFFFF) + I16_MIN).astype(I16)
        return carry

    lax.fori_loop(0, n_att, score_tile, 0)

    def count(pred):
        def body(g, c):
            off = pl.multiple_of(g * COUNT_TILE, COUNT_TILE)
            hit = pred(skey_s[pl.ds(off, COUNT_TILE), :], off + row_cnt)
            return c + _fold8(jnp.where(hit, 1, 0).astype(I32), jnp.add)
        c = lax.fori_loop(0, n_cnt, body, jnp.zeros((SUBLANES, Q_BLOCK), I32))
        return jnp.sum(c, axis=0, keepdims=True)

    def spread16(v):
        return jnp.broadcast_to(v, (COUNT_TILE, Q_BLOCK)).astype(I16)

    def threshold(n):
        def tiles(ref):
            return [ref[g * COUNT_TILE:(g + 1) * COUNT_TILE, :] for g in range(n)]

        def count16(ref, pred):
            hits = jnp.concatenate(
                [jnp.where(pred(v), jnp.int16(1), jnp.int16(0)) for v in tiles(ref)], axis=0)
            c = _fold_rows(hits, jnp.add, PACK16)
            return jnp.sum(c.astype(I32), axis=0, keepdims=True)

        def kth_largest16(ref, kth):
            def step(it, t):
                cand = t + lax.shift_left(jnp.int32(1), 15 - it)
                cand16 = spread16(cand)
                return jnp.where(count16(ref, lambda v: v >= cand16) >= kth, cand, t)
            return lax.fori_loop(0, 16, step, jnp.full((1, Q_BLOCK), I16_MIN, I32))

        t_hi = kth_largest16(hi_s, k_top)
        t_hi16 = spread16(t_hi)
        n_above = count16(hi_s, lambda v: v > t_hi16)
        for g in range(n):
            rows = slice(g * COUNT_TILE, (g + 1) * COUNT_TILE)
            lo_s[rows, :] = jnp.where(hi_s[rows, :] == t_hi16, lo_s[rows, :], jnp.int16(I16_MIN))
        t_lo = kth_largest16(lo_s, k_top - n_above)
        t_lo16 = spread16(t_lo)
        n_ge = n_above + count16(lo_s, lambda v: v >= t_lo16)
        return lax.shift_left(t_hi, 16) | (t_lo - I16_MIN), n_ge

    t, n_ge = lax.switch(n_cnt - 1, [functools.partial(threshold, n)
                                     for n in range(1, seq // COUNT_TILE + 1)])


    @pl.when(jnp.max(n_ge) > k_top)
    def _():
        n_gt = count(lambda key, pos: key > t)
        r_m1 = k_top - n_gt - 1

        def pos_search(it, j):
            cand = j | lax.shift_left(jnp.int32(1), pos_bits - 1 - it)
            c = count(lambda key, pos: (key == t) & (pos < cand))
            return jnp.where(c <= r_m1, cand, j)

        j_cut = lax.fori_loop(0, pos_bits, pos_search, jnp.zeros((1, Q_BLOCK), I32))

        def drop(g, carry):
            off = pl.multiple_of(g * KEY_TILE, KEY_TILE)
            key = skey_s[pl.ds(off, KEY_TILE), :]
            cut = (key == t) & ((off + row) > j_cut)
            skey_s[pl.ds(off, KEY_TILE), :] = jnp.where(cut, INT_MIN, key)
            return carry

        lax.fori_loop(0, n_att, drop, 0)

    t_sel = jnp.maximum(t, INT_MIN + 1)

    def mask_tile(g, carry):
        off = pl.multiple_of(g * KEY_TILE, KEY_TILE)
        mb_s[pl.ds(off, KEY_TILE), :] = jnp.where(
            skey_s[pl.ds(off, KEY_TILE), :] >= t_sel, 0.0, NEG_BIG)
        return carry

    lax.fori_loop(0, n_att, mask_tile, 0)

    acc_s[...] = jnp.zeros(acc_s.shape, F32)

    def attend_tile(g, carry, table):
        m_run, l8 = carry
        off = pl.multiple_of(g * KEY_TILE, KEY_TILE)
        kp_t = kp_s[pl.ds(off, KEY_TILE), :]
        vt_t = vt_s[:, pl.ds(off, KEY_TILE)]
        mb = mb_s[pl.ds(off, KEY_TILE), :]
        m_out, l_out, lgs = [], [], []
        for step in range(n_dots + QK_LEAD):
            if step < n_dots:
                lgs.append(lax.dot_general(
                    kp_t, q_s[step * dot_w:(step + 1) * dot_w, :], _NT_DIMS,
                    preferred_element_type=F32))
            c = step - QK_LEAD
            if c < 0:
                continue
            lg = lgs[c]
            ps, alphas = [], []
            for s in range(HEADS_PER_DOT):
                h = c * HEADS_PER_DOT + s
                x = lg[:, s * Q_BLOCK:(s + 1) * Q_BLOCK] + mb
                if table is not None:
                    x = x + nb_ref[table, h]
                m_new = jnp.maximum(m_run[h], jnp.max(_fold8(x, jnp.maximum), axis=0, keepdims=True))
                alpha = jnp.exp2(m_run[h] - m_new)
                p = jnp.exp2(x - m_new)
                m_out.append(m_new)
                l_out.append(alpha * l8[h] + _fold8(p, jnp.add))
                alphas.append(alpha)
                ps.append(p.astype(BF16))
            acc_s[c] = (acc_s[c] * jnp.concatenate(alphas, axis=1)
                        + jnp.dot(vt_t, jnp.concatenate(ps, axis=1), preferred_element_type=F32))
        return tuple(m_out), tuple(l_out)

    n_near = jnp.where((i & 1) == 0, jnp.minimum(n_att, 2), 1)
    n_far = n_att - n_near
    carry = (tuple(jnp.full((1, Q_BLOCK), NEG_BIG, F32) for _ in range(n_heads)),
             tuple(jnp.zeros((SUBLANES, Q_BLOCK), F32) for _ in range(n_heads)))
    carry = lax.fori_loop(0, n_far, lambda g, cr: attend_tile(g, cr, None), carry)
    _, l8 = lax.fori_loop(n_far, n_att, lambda g, cr: attend_tile(g, cr, 2 * g - i + 2), carry)

    for c in range(n_dots):
        acc = acc_s[c]
        for s in range(HEADS_PER_DOT):
            h = c * HEADS_PER_DOT + s
            denom = jnp.sum(l8[h], axis=0, keepdims=True)
            ot_s[h * HEAD_DIM:(h + 1) * HEAD_DIM, :] = acc[:, s * Q_BLOCK:(s + 1) * Q_BLOCK] / denom
    o_ref[...] = ot_s[...].T.astype(o_ref.dtype)


def _attention(zs, zb, near_bias, far_q, batch, seq, n_heads, qi_block):
    d_q = n_heads * HEAD_DIM
    nqb = seq // Q_BLOCK
    k_top = min(INDEX_TOPK, seq // 4)
    assert seq % COUNT_TILE == 0 and n_heads % HEADS_PER_DOT == 0
    kern = functools.partial(_attn_kernel, k_top=k_top, n_heads=n_heads,
                             pos_bits=max(1, (seq - 1).bit_length()))
    return pl.pallas_call(
        kern,
        grid=(batch, nqb),
        in_specs=[
            pl.BlockSpec((seq, 2 * LANES), lambda b, i: (b, 0)),
            pl.BlockSpec((Q_BLOCK, d_q), lambda b, i: (b * nqb + i, 0)),
            pl.BlockSpec((Q_BLOCK, IDX_HEADS * IDX_DIM), lambda b, i: (b * nqb + i, qi_block)),
            pl.BlockSpec(far_q.shape, lambda b, i: (0, 0)),
            pl.BlockSpec(near_bias.shape, lambda b, i: (0, 0, 0, 0)),
        ],
        out_specs=pl.BlockSpec((Q_BLOCK, d_q), lambda b, i: (b * nqb + i, 0)),
        out_shape=jax.ShapeDtypeStruct((batch * seq, d_q), BF16),
        scratch_shapes=[
            pltpu.VMEM((seq, LANES), BF16),
            pltpu.VMEM((seq, IDX_DIM), BF16),
            pltpu.VMEM((KV_DIM, seq), BF16),
            pltpu.VMEM((n_heads * Q_BLOCK, LANES), BF16),
            pltpu.VMEM((IDX_HEADS * Q_BLOCK, IDX_DIM), BF16),
            pltpu.VMEM((seq, Q_BLOCK), I32),
            pltpu.VMEM((seq, Q_BLOCK), I16),
            pltpu.VMEM((seq, Q_BLOCK), I16),
            pltpu.VMEM((seq, Q_BLOCK), F32),
            pltpu.VMEM((n_heads // HEADS_PER_DOT, KV_DIM, HEADS_PER_DOT * Q_BLOCK), F32),
            pltpu.VMEM((d_q, Q_BLOCK), F32),
        ],
        compiler_params=_params(2),
        name="dsa_attention",
    )(zs, zb, zb, far_q, near_bias)


def _t5_bucket(rel):
    half = REL_BUCKETS // 2
    max_exact = half // 2
    ret = jnp.where(rel > 0, half, 0)
    n = jnp.abs(rel)
    nf = jnp.maximum(n, 1).astype(F32)
    large = max_exact + (jnp.log(nf / max_exact) / math.log(REL_MAX_DIST / max_exact)
                         * (half - max_exact)).astype(I32)
    large = jnp.minimum(large, half - 1)
    return ret + jnp.where(n < max_exact, n, large)


def _bias_tables(rel_bias):
    n_heads = rel_bias.shape[1]
    far = rel_bias[_t5_bucket(jnp.asarray(-2 * REL_MAX_DIST, I32))].astype(F32) * LOG2E
    terms, rest = [], far
    for _ in range(N_BIAS_LANES):
        terms.append(rest.astype(BF16).astype(F32))
        rest = rest - terms[-1]
    far_q = jnp.zeros((n_heads, LANES - HEAD_DIM), F32).at[:, :N_BIAS_LANES].set(
        jnp.stack(terms, axis=1))
    far_sum = sum(terms[1:], terms[0])
    delta = (np.arange(3)[:, None, None] - 2) * Q_BLOCK
    rel = delta + np.arange(KEY_TILE)[None, :, None] - np.arange(Q_BLOCK)[None, None, :]
    near = rel_bias[_t5_bucket(jnp.asarray(rel, I32))].astype(F32) * LOG2E
    near = jnp.moveaxis(near, -1, 1) - far_sum[None, :, None, None]
    return near, far_q


def _neg_expm1(y):
    series = -y * (1.0 + y * (1.0 / 2 + y * (1.0 / 6 + y * (1.0 / 24 + y * (1.0 / 120)))))
    return jnp.where(y > -0.1, series, 1.0 - jnp.exp(y))


def _rglru_kernel(xr_ref, yr_ref, cw_ref, cb_ref, wa_ref, ba_ref, wx_ref, bx_ref, lam_ref,
                  o_ref, ext_s, a_s, u_s, h_s, *, ts, bw):
    j = pl.program_id(1)

    @pl.when(j == 0)
    def _():
        ext_s[0:SUBLANES, :] = jnp.zeros((SUBLANES, ext_s.shape[1]), F32)
        h_s[...] = jnp.zeros(h_s.shape, F32)

    x = xr_ref[...].astype(F32)
    ext_s[SUBLANES:SUBLANES + ts, :] = x
    xc = ext_s[SUBLANES - 3:SUBLANES - 3 + ts, :] * cw_ref[0:1, :]
    xc = xc + ext_s[SUBLANES - 2:SUBLANES - 2 + ts, :] * cw_ref[1:2, :]
    xc = xc + ext_s[SUBLANES - 1:SUBLANES - 1 + ts, :] * cw_ref[2:3, :]
    xc = xc + x * cw_ref[3:4, :]
    xc = xc + cb_ref[...]
    ext_s[0:SUBLANES, :] = ext_s[ts:ts + SUBLANES, :]

    sp = jax.nn.softplus(-lam_ref[...])
    for n in range(RNN_BLOCKS):
        cs = slice(n * bw, (n + 1) * bw)
        xb = xc[:, cs]
        xb16 = xb.astype(BF16)
        r = jax.nn.sigmoid(jnp.dot(xb16, wa_ref[n], preferred_element_type=F32) + ba_ref[:, cs])
        g = jax.nn.sigmoid(jnp.dot(xb16, wx_ref[n], preferred_element_type=F32) + bx_ref[:, cs])
        log_a = -LRU_C * r * sp[:, cs]
        a_s[:, cs] = jnp.exp(log_a)
        u_s[:, cs] = jnp.sqrt(_neg_expm1(2.0 * log_a)) * (g * xb)

    def step(t, h):
        h = a_s[pl.ds(t, 1), :] * h + u_s[pl.ds(t, 1), :]
        u_s[pl.ds(t, 1), :] = h
        return h

    h_s[...] = lax.fori_loop(0, ts, step, h_s[...], unroll=8)
    o_ref[...] = (u_s[...] * jax.nn.gelu(yr_ref[...].astype(F32))).astype(o_ref.dtype)


def _rglru(zb, conv_w, conv_b, wa, ba, wx, bx, lam, batch, seq, width, xr_block, yr_block, ts):
    nts = seq // ts
    bw = width // RNN_BLOCKS
    row = lambda v: v.reshape(1, width)
    full = lambda shape: pl.BlockSpec(shape, lambda b, j: (0,) * len(shape))
    return pl.pallas_call(
        functools.partial(_rglru_kernel, ts=ts, bw=bw),
        grid=(batch, nts),
        in_specs=[pl.BlockSpec((ts, width), lambda b, j: (b * nts + j, xr_block)),
                  pl.BlockSpec((ts, width), lambda b, j: (b * nts + j, yr_block)),
                  full((CONV_W, width)), full((1, width)),
                  full((RNN_BLOCKS, bw, bw)), full((1, width)),
                  full((RNN_BLOCKS, bw, bw)), full((1, width)), full((1, width))],
        out_specs=pl.BlockSpec((ts, width), lambda b, j: (b * nts + j, 0)),
        out_shape=jax.ShapeDtypeStruct((batch * seq, width), BF16),
        scratch_shapes=[pltpu.VMEM((ts + 2 * SUBLANES, width), F32),
                        pltpu.VMEM((ts, width), F32),
                        pltpu.VMEM((ts, width), F32),
                        pltpu.VMEM((1, width), F32)],
        compiler_params=_params(2),
        name="rglru",
    )(zb, zb, conv_w, row(conv_b), wa.astype(BF16), row(ba), wx.astype(BF16), row(bx), row(lam))


def _layer_norm(y, g, b):
    mu = jnp.mean(y, axis=-1, keepdims=True)
    var = jnp.mean(jnp.square(y - mu), axis=-1, keepdims=True)
    return (y - mu) * lax.rsqrt(var + LN_EPS) * g + b


def _merge_kernel(oa_ref, or_ref, ga_ref, gr_ref, x_ref, wba_ref, wbr_ref, wo_ref, g_ref, b_ref,
                  x1_ref, x1b_ref, *, alpha):
    a = jnp.dot(oa_ref[...], wba_ref[...], preferred_element_type=F32)
    r = jnp.dot(or_ref[...], wbr_ref[...], preferred_element_type=F32)
    merged = (jax.nn.sigmoid(ga_ref[...].astype(F32)) * a
              + jax.nn.sigmoid(gr_ref[...].astype(F32)) * r)
    mix = jnp.dot(merged.astype(BF16), wo_ref[...], preferred_element_type=F32)
    x1 = _layer_norm(alpha * x_ref[...] + mix, g_ref[...], b_ref[...])
    x1_ref[...] = x1
    x1b_ref[...] = x1.astype(BF16)


def _merge_ln(oa, orn, zb, x, wba, wbr, wo, g, b, alpha, ga_block, gr_block, tm):
    m, d = x.shape
    tok = lambda blk: pl.BlockSpec((tm, d), lambda i: (i, blk))
    full = lambda shape: pl.BlockSpec(shape, lambda i: (0,) * len(shape))
    return pl.pallas_call(
        functools.partial(_merge_kernel, alpha=alpha),
        grid=(m // tm,),
        in_specs=[tok(0), tok(0), tok(ga_block), tok(gr_block), tok(0),
                  full((d, d)), full((d, d)), full((d, d)), full((1, d)), full((1, d))],
        out_specs=[tok(0), tok(0)],
        out_shape=[jax.ShapeDtypeStruct((m, d), F32), jax.ShapeDtypeStruct((m, d), BF16)],
        compiler_params=_params(1),
        name="merge_ln1",
    )(oa, orn, zb, zb, x, wba, wbr, wo, g.reshape(1, d), b.reshape(1, d))


def _ple_kernel(*refs, alpha, routed):
    if routed:
        (x1_ref, x1b_ref, p_ref, pg_ref, pp_ref, g_ref, b_ref,
         ya_ref, yb_ref, gt_ref, x2_ref, x2b_ref) = refs
        f = (gt_ref[:, 0:1] * ya_ref[...].astype(F32)
             + gt_ref[:, 1:2] * yb_ref[...].astype(F32))
    else:
        x1_ref, x1b_ref, p_ref, pg_ref, pp_ref, g_ref, b_ref, f_ref, x2_ref, x2b_ref = refs
        f = f_ref[...].astype(F32)
    gate = jax.nn.sigmoid(jnp.dot(x1b_ref[...], pg_ref[...], preferred_element_type=F32))
    proj = jnp.dot(p_ref[...].astype(BF16), pp_ref[...], preferred_element_type=F32)
    x2 = _layer_norm(alpha * x1_ref[...] + f + gate * proj, g_ref[...], b_ref[...])
    x2_ref[...] = x2
    x2b_ref[...] = x2.astype(BF16)


def _ple_ln(x1, x1b, p_all, layer, pg, pp, g, b, f_parts, alpha, tm):
    m, d = x1.shape
    pd = p_all.shape[1]
    nt = m // tm
    tok = pl.BlockSpec((tm, d), lambda i: (i, 0))
    full = lambda shape: pl.BlockSpec(shape, lambda i: (0,) * len(shape))
    routed = len(f_parts) == 3
    f_specs = [tok, tok, pl.BlockSpec((tm, LANES), lambda i: (i, 0))] if routed else [tok]
    return pl.pallas_call(
        functools.partial(_ple_kernel, alpha=alpha, routed=routed),
        grid=(nt,),
        in_specs=[tok, tok, pl.BlockSpec((tm, pd), lambda i: (layer * nt + i, 0)),
                  full((d, d)), full((pd, d)), full((1, d)), full((1, d))] + f_specs,
        out_specs=[tok, tok],
        out_shape=[jax.ShapeDtypeStruct((m, d), F32), jax.ShapeDtypeStruct((m, d), BF16)],
        compiler_params=_params(1),
        name="ple_ln2",
    )(x1, x1b, p_all, pg, pp, g.reshape(1, d), b.reshape(1, d), *f_parts)


def _ffn_kernel(te_ref, nt_ref, x_ref, wg_ref, wu_ref, wd_ref, o_ref, acc_s):
    t = pl.program_id(0)
    j = pl.program_id(1)
    last = pl.num_programs(1) - 1
    live = t < nt_ref[0]

    @pl.when(live)
    def _():
        x = x_ref[...]
        gate = jnp.dot(x, wg_ref[0], preferred_element_type=F32)
        up = jnp.dot(x, wu_ref[0], preferred_element_type=F32)
        hid = (jax.nn.silu(gate) * up).astype(BF16)
        part = jnp.dot(hid, wd_ref[0], preferred_element_type=F32)

        @pl.when(j == 0)
        def _():
            acc_s[...] = part

        @pl.when(j > 0)
        def _():
            acc_s[...] += part

    @pl.when(j == last)
    def _():
        o_ref[...] = jnp.where(live, acc_s[...], 0.0).astype(o_ref.dtype)


def _ffn(x, tile_expert, n_live, wg, wu, wd, tm, tf):
    rows, d = x.shape
    f = wg.shape[2]
    fcol = lambda t, j, te, nt: jnp.where(t < nt[0], j, 0)
    grid_spec = pltpu.PrefetchScalarGridSpec(
        num_scalar_prefetch=2,
        grid=(rows // tm, f // tf),
        in_specs=[pl.BlockSpec((tm, d), lambda t, j, te, nt: (t, 0)),
                  pl.BlockSpec((1, d, tf), lambda t, j, te, nt: (te[t], 0, fcol(t, j, te, nt))),
                  pl.BlockSpec((1, d, tf), lambda t, j, te, nt: (te[t], 0, fcol(t, j, te, nt))),
                  pl.BlockSpec((1, tf, d), lambda t, j, te, nt: (te[t], fcol(t, j, te, nt), 0))],
        out_specs=pl.BlockSpec((tm, d), lambda t, j, te, nt: (t, 0)),
        scratch_shapes=[pltpu.VMEM((tm, d), F32)])
    return pl.pallas_call(
        _ffn_kernel, grid_spec=grid_spec,
        out_shape=jax.ShapeDtypeStruct((rows, d), BF16),
        compiler_params=_params(2),
        name="swiglu_ffn",
    )(tile_expert, n_live, x, wg, wu, wd)


def _router_kernel(x_ref, w_ref, b_ref, idx_ref, gate_ref):
    lg = jnp.dot(x_ref[...], w_ref[...], preferred_element_type=F32) + b_ref[...]
    lane = lax.broadcasted_iota(I32, lg.shape, 1)
    v1 = jnp.max(lg, axis=1, keepdims=True)
    i1 = jnp.min(jnp.where(lg == v1, lane, LANES), axis=1, keepdims=True)
    lg2 = jnp.where(lane == i1, -jnp.inf, lg)
    v2 = jnp.max(lg2, axis=1, keepdims=True)
    i2 = jnp.min(jnp.where(lg2 == v2, lane, LANES), axis=1, keepdims=True)
    e2 = jnp.exp(v2 - v1)
    denom = 1.0 + e2
    idx_ref[...] = jnp.where(lane == 0, i1, jnp.where(lane == 1, i2, 0))
    gate_ref[...] = jnp.where(lane == 0, 1.0 / denom, jnp.where(lane == 1, e2 / denom, 0.0))


def _router(x1, router, router_b, tm):
    m, d = x1.shape
    w = jnp.zeros((d, LANES), F32).at[:, :N_EXPERTS].set(router.astype(F32))
    b = jnp.full((1, LANES), -jnp.inf, F32).at[0, :N_EXPERTS].set(router_b.astype(F32))
    return pl.pallas_call(
        _router_kernel,
        grid=(m // tm,),
        in_specs=[pl.BlockSpec((tm, d), lambda i: (i, 0)),
                  pl.BlockSpec((d, LANES), lambda i: (0, 0)),
                  pl.BlockSpec((1, LANES), lambda i: (0, 0))],
        out_specs=[pl.BlockSpec((tm, LANES), lambda i: (i, 0)),
                   pl.BlockSpec((tm, LANES), lambda i: (i, 0))],
        out_shape=[jax.ShapeDtypeStruct((m, LANES), I32), jax.ShapeDtypeStruct((m, LANES), F32)],
        compiler_params=_params(1),
        name="router_top2",
    )(x1, w, b)


def _dispatch_plan(top_i, tm):
    m = top_i.shape[0]
    n_assign = m * TOP_K
    e_flat = top_i.reshape(n_assign)
    onehot = (e_flat[:, None] == jnp.arange(N_EXPERTS, dtype=I32)[None, :]).astype(I32)
    csum = jnp.cumsum(onehot, axis=0)
    counts = csum[-1]
    rank = jnp.sum(csum * onehot, axis=1) - 1
    padded = ((counts + tm - 1) // tm) * tm
    ends = jnp.cumsum(padded)
    pos = (ends - padded)[e_flat] + rank
    n_rows = n_assign + N_EXPERTS * tm
    src = jnp.zeros((n_rows,), I32).at[pos].set(jnp.arange(n_assign, dtype=I32) // TOP_K)
    tile_start = jnp.arange(n_rows // tm, dtype=I32) * tm
    tile_expert = jnp.minimum(jnp.searchsorted(ends, tile_start, side="right"),
                              N_EXPERTS - 1).astype(I32)
    n_live = (ends[-1] // tm).astype(I32).reshape(1)
    return src, pos.reshape(m, TOP_K), tile_expert, n_live


def _pick(n, candidates):
    for c in candidates:
        if n % c == 0:
            return c
    return n


def _forward(x, p, w_in, conv_w, conv_b, lru_wa, lru_ba, lru_wx, lru_bx, lru_lam, w_br_attn,
             w_br_rnn, w_o, rel_bias, ln1_g, ln1_b, ffn_w_gate, ffn_w_up, ffn_w_down, moe_router,
             moe_router_b, moe_w_gate, moe_w_up, moe_w_down, ple_w_gate, ple_w_proj, ln2_g, ln2_b):
    batch, seq, d = x.shape
    depth = w_in.shape[0]
    m = batch * seq
    n_heads = d // HEAD_DIM
    alpha = (2 * depth) ** 0.25
    d_q, d_qi = n_heads * HEAD_DIM, IDX_HEADS * IDX_DIM
    assert d_q == d and w_br_rnn.shape[1] == d and (d_q + 4 * d) % d_qi == 0

    sizes = (d_q, KV_DIM, KV_DIM, d_qi, IDX_DIM, IDX_HEADS, d, d, d, d)
    offs = np.concatenate([[0], np.cumsum(sizes)])
    col = lambda w, k: w[:, offs[k]:offs[k + 1]]
    q_scale = HEAD_DIM ** -0.5 * LOG2E
    big_scale = jnp.concatenate([jnp.full((1, d_q), q_scale, F32),
                                 jnp.ones((1, 4 * d + d_qi), F32)], axis=1)
    small_scale = jnp.ones((1, 2 * LANES), F32)
    xr_block, yr_block, ga_block, gr_block = 1, 2, 3, 4
    qi_block = (d_q + 4 * d) // d_qi

    near_bias, far_q = _bias_tables(rel_bias)

    tm = _pick(m, (1024, 512, 256, 128))
    tn_big = _pick(d_q + 4 * d + d_qi, (512, 256, 128))
    ts = _pick(seq, (256, 128))
    tm_res = _pick(m, (512, 256, 128))

    xf = x.reshape(m, d)
    xb = xf.astype(BF16)
    p_all = p.reshape(depth * m, p.shape[-1])

    for i in range(depth):
        w = w_in[i]
        w_big = jnp.concatenate([col(w, 0), col(w, 6), col(w, 7), col(w, 8), col(w, 9), col(w, 3)],
                                axis=1).astype(BF16)
        w_small = jnp.concatenate(
            [col(w, 1), col(w, 2), col(w, 4), col(w, 5),
             jnp.zeros((d, 2 * LANES - 2 * KV_DIM - IDX_DIM - IDX_HEADS), w.dtype)], axis=1).astype(BF16)
        zb = _matmul(xb, w_big, big_scale, BF16, tm, tn_big)
        zs = _matmul(xb, w_small, small_scale, F32, tm, 2 * LANES)

        o_attn = _attention(zs, zb, near_bias, far_q, batch, seq, n_heads, qi_block)
        o_rnn = _rglru(zb, conv_w[i], conv_b[i], lru_wa[i], lru_ba[i], lru_wx[i], lru_bx[i],
                       lru_lam[i], batch, seq, d, xr_block, yr_block, ts)
        x1, x1b = _merge_ln(o_attn, o_rnn, zb, xf, w_br_attn[i].astype(BF16),
                            w_br_rnn[i].astype(BF16), w_o[i].astype(BF16), ln1_g[i], ln1_b[i],
                            alpha, ga_block, gr_block, tm_res)

        j = i // 2
        if i % 2 == 0:
            f_dim = ffn_w_gate.shape[-1]
            tf = _pick(f_dim, (512, f_dim // 2))
            n_tiles = m // tm
            f_parts = [_ffn(x1b, jnp.zeros((n_tiles,), I32), jnp.full((1,), n_tiles, I32),
                            ffn_w_gate[j][None].astype(BF16), ffn_w_up[j][None].astype(BF16),
                            ffn_w_down[j][None].astype(BF16), tm, tf)]
        else:
            f_dim = moe_w_gate.shape[-1]
            tf = _pick(f_dim, (512, f_dim // 2))
            idx, gates = _router(x1, moe_router[j], moe_router_b[j], tm_res)
            src, pos, tile_expert, n_live = _dispatch_plan(idx[:, :TOP_K], tm)
            y = _ffn(x1b[src], tile_expert, n_live, moe_w_gate[j].astype(BF16),
                     moe_w_up[j].astype(BF16), moe_w_down[j].astype(BF16), tm, tf)
            f_parts = [y[pos[:, 0]], y[pos[:, 1]], gates]

        xf, xb = _ple_ln(x1, x1b, p_all, i, ple_w_gate[i].astype(BF16), ple_w_proj[i].astype(BF16),
                         ln2_g[i], ln2_b[i], f_parts, alpha, tm_res)

    return xf.reshape(batch, seq, d)


@jax.jit
def kernel(x, p, w_in, conv_w, conv_b, lru_wa, lru_ba, lru_wx, lru_bx, lru_lam, w_br_attn, w_br_rnn,
           w_o, rel_bias, ln1_g, ln1_b, ffn_w_gate, ffn_w_up, ffn_w_down, moe_router, moe_router_b,
           moe_w_gate, moe_w_up, moe_w_down, ple_w_gate, ple_w_proj, ln2_g, ln2_b):
    return _forward(x, p, w_in, conv_w, conv_b, lru_wa, lru_ba, lru_wx, lru_bx, lru_lam, w_br_attn,
                    w_br_rnn, w_o, rel_bias, ln1_g, ln1_b, ffn_w_gate, ffn_w_up, ffn_w_down,
                    moe_router, moe_router_b, moe_w_gate, moe_w_up, moe_w_down, ple_w_gate,
                    ple_w_proj, ln2_g, ln2_b)
```

```python
import functools
import math

import jax
import jax.numpy as jnp
import numpy as np
from jax import lax
from jax.experimental import pallas as pl
from jax.experimental.pallas import tpu as pltpu

CHUNK = 64
Q_BLOCK = 128
HEAD_DIM = 64
KV_DIM = 64
IDX_HEADS = 8
IDX_DIM = 64
INDEX_TOPK = 256
RNN_BLOCKS = 8
CONV_W = 4
LRU_C = 8.0
REL_BUCKETS = 32
REL_MAX_DIST = 128
N_EXPERTS = 8
TOP_K = 2
LN_EPS = 1e-5

LANES = 128
SUBLANES = 8
VMEM_LIMIT_BYTES = 56 * 1024 * 1024

KEY_TILE = 2 * Q_BLOCK
COUNT_TILE = 2 * KEY_TILE
HEADS_PER_DOT = 2
N_BIAS_LANES = 3
DOWN_SPLIT = 4
INT_MIN = -(2 ** 31)
I16_MIN = -(2 ** 15)
PACK16 = 2 * SUBLANES
NEG_BIG = -1e30
LOG2E = math.log2(math.e)

BF16 = jnp.bfloat16
F32 = jnp.float32
I32 = jnp.int32
I16 = jnp.int16

_NT_DIMS = (((1,), (1,)), ((), ()))


def _params(n_axes):
    return pltpu.CompilerParams(
        dimension_semantics=("arbitrary",) * n_axes,
        vmem_limit_bytes=VMEM_LIMIT_BYTES)


def _mm_kernel(x_ref, w_ref, s_ref, o_ref):
    acc = jnp.dot(x_ref[...], w_ref[...], preferred_element_type=F32)
    o_ref[...] = (acc * s_ref[...]).astype(o_ref.dtype)


def _matmul(x, w, scale, out_dtype, tm, tn):
    m, k = x.shape
    n = w.shape[1]
    return pl.pallas_call(
        _mm_kernel,
        grid=(m // tm, n // tn),
        in_specs=[pl.BlockSpec((tm, k), lambda i, j: (i, 0)),
                  pl.BlockSpec((k, tn), lambda i, j: (0, j)),
                  pl.BlockSpec((1, tn), lambda i, j: (0, j))],
        out_specs=pl.BlockSpec((tm, tn), lambda i, j: (i, j)),
        out_shape=jax.ShapeDtypeStruct((m, n), out_dtype),
        compiler_params=_params(2),
        name="proj_in",
    )(x, w, scale)


def _fold_rows(x, op, rows):
    parts = [x[r * rows:(r + 1) * rows, :] for r in range(x.shape[0] // rows)]
    while len(parts) > 1:
        nxt = [op(parts[j], parts[j + 1]) for j in range(0, len(parts) - 1, 2)]
        if len(parts) % 2:
            nxt.append(parts[-1])
        parts = nxt
    return parts[0]


def _fold8(x, op):
    return _fold_rows(x, op, SUBLANES)


def _attn_kernel(zs_ref, q_ref, qi_ref, fbq_ref, nb_ref, o_ref,
                 kp_s, ki_s, vt_s, q_s, qi_s, skey_s, hi_s, lo_s, mb_s, l_s, acc_s, ot_s,
                 *, k_top, n_heads, pos_bits):
    b = pl.program_id(0)
    i = pl.program_id(1)
    seq = zs_ref.shape[0]
    n_dots = n_heads // HEADS_PER_DOT
    dot_w = HEADS_PER_DOT * Q_BLOCK
    start = pl.multiple_of(i * Q_BLOCK, Q_BLOCK)
    n_att = (i + 2) // 2
    n_cnt = (n_att + 1) // 2

    @pl.when((b == 0) & (i == 0))
    def _():
        for h in range(n_heads):
            q_s[h * Q_BLOCK:(h + 1) * Q_BLOCK, HEAD_DIM:] = jnp.broadcast_to(
                fbq_ref[h:h + 1, :], (Q_BLOCK, LANES - HEAD_DIM)).astype(BF16)

    @pl.when(i == 0)
    def _():
        kv = zs_ref[:, 0:2 * KV_DIM]
        col = lax.broadcasted_iota(I32, kv.shape, 1)
        ones = jnp.where(col < KV_DIM + N_BIAS_LANES, 1.0, 0.0)
        kp_s[...] = jnp.where(col < KV_DIM, kv, ones).astype(BF16)
        vt_s[...] = kv.T[KV_DIM:2 * KV_DIM, :].astype(BF16)
        ki_s[...] = zs_ref[:, 2 * KV_DIM:2 * KV_DIM + IDX_DIM].astype(BF16)
        skey_s[...] = jnp.full(skey_s.shape, INT_MIN, I32)
        hi_s[...] = jnp.full(hi_s.shape, I16_MIN, I16)
        lo_s[...] = jnp.full(lo_s.shape, I16_MIN, I16)

    for h in range(n_heads):
        q_s[h * Q_BLOCK:(h + 1) * Q_BLOCK, 0:HEAD_DIM] = q_ref[:, h * HEAD_DIM:(h + 1) * HEAD_DIM]
    for h in range(IDX_HEADS):
        qi_s[h * Q_BLOCK:(h + 1) * Q_BLOCK, :] = qi_ref[:, h * IDX_DIM:(h + 1) * IDX_DIM]

    w_t = zs_ref[pl.ds(start, Q_BLOCK), LANES:2 * LANES].T

    row = lax.broadcasted_iota(I32, (KEY_TILE, Q_BLOCK), 0)
    lane = lax.broadcasted_iota(I32, (KEY_TILE, Q_BLOCK), 1)
    row_cnt = lax.broadcasted_iota(I32, (COUNT_TILE, Q_BLOCK), 0)
    q_chunk = (start + lane) // CHUNK

    def score_tile(g, carry):
        off = pl.multiple_of(g * KEY_TILE, KEY_TILE)
        ki_t = ki_s[pl.ds(off, KEY_TILE), :]
        acc = jnp.zeros((KEY_TILE, Q_BLOCK), F32)
        for c in range(IDX_HEADS // HEADS_PER_DOT):
            d = lax.dot_general(ki_t, qi_s[c * dot_w:(c + 1) * dot_w, :], _NT_DIMS,
                                preferred_element_type=F32)
            for s in range(HEADS_PER_DOT):
                h = c * HEADS_PER_DOT + s
                acc = acc + (jnp.maximum(d[:, s * Q_BLOCK:(s + 1) * Q_BLOCK], 0.0)
                             * w_t[IDX_DIM + h:IDX_DIM + h + 1, :])
        bits = lax.bitcast_convert_type(acc, I32)
        key = bits ^ ((bits >> 31) & jnp.int32(0x7FFFFFFF))
        k_chunk = (off + row) // CHUNK
        key = jnp.where(k_chunk <= q_chunk, key, INT_MIN)
        skey_s[pl.ds(off, KEY_TILE), :] = key
        hi_s[pl.ds(off, KEY_TILE), :] = (key >> 16).astype(I16)
        lo_s[pl.ds(off, KEY_TILE), :] = ((key & 0xFFFF) + I16_MIN).astype(I16)
        return carry

    lax.fori_loop(0, n_att, score_tile, 0)

    def count(pred):
        def body(g, c):
            off = pl.multiple_of(g * COUNT_TILE, COUNT_TILE)
            hit = pred(skey_s[pl.ds(off, COUNT_TILE), :], off + row_cnt)
            return c + _fold8(jnp.where(hit, 1, 0).astype(I32), jnp.add)
        c = lax.fori_loop(0, n_cnt, body, jnp.zeros((SUBLANES, Q_BLOCK), I32))
        return jnp.sum(c, axis=0, keepdims=True)

    def spread16(v):
        return jnp.broadcast_to(v, (COUNT_TILE, Q_BLOCK)).astype(I16)

    def threshold(n):
        def tiles(ref):
            return [ref[g * COUNT_TILE:(g + 1) * COUNT_TILE, :] for g in range(n)]

        def count16(ref, pred):
            hits = jnp.concatenate(
                [jnp.where(pred(v), jnp.int16(1), jnp.int16(0)) for v in tiles(ref)], axis=0)
            c = _fold_rows(hits, jnp.add, PACK16)
            return jnp.sum(c.astype(I32), axis=0, keepdims=True)

        def kth_largest16(ref, kth):
            def step(it, t):
                cand = t + lax.shift_left(jnp.int32(1), 15 - it)
                cand16 = spread16(cand)
                return jnp.where(count16(ref, lambda v: v >= cand16) >= kth, cand, t)
            return lax.fori_loop(0, 16, step, jnp.full((1, Q_BLOCK), I16_MIN, I32))

        t_hi = kth_largest16(hi_s, k_top)
        t_hi16 = spread16(t_hi)
        n_above = count16(hi_s, lambda v: v > t_hi16)
        for g in range(n):
            rows = slice(g * COUNT_TILE, (g + 1) * COUNT_TILE)
            lo_s[rows, :] = jnp.where(hi_s[rows, :] == t_hi16, lo_s[rows, :], jnp.int16(I16_MIN))
        t_lo = kth_largest16(lo_s, k_top - n_above)
        t_lo16 = spread16(t_lo)
        n_ge = n_above + count16(lo_s, lambda v: v >= t_lo16)
        return lax.shift_left(t_hi, 16) | (t_lo - I16_MIN), n_ge

    t, n_ge = lax.switch(n_cnt - 1, [functools.partial(threshold, n)
                                     for n in range(1, seq // COUNT_TILE + 1)])


    @pl.when(jnp.max(n_ge) > k_top)
    def _():
        n_gt = count(lambda key, pos: key > t)
        r_m1 = k_top - n_gt - 1

        def pos_search(it, j):
            cand = j | lax.shift_left(jnp.int32(1), pos_bits - 1 - it)
            c = count(lambda key, pos: (key == t) & (pos < cand))
            return jnp.where(c <= r_m1, cand, j)

        j_cut = lax.fori_loop(0, pos_bits, pos_search, jnp.zeros((1, Q_BLOCK), I32))

        def drop(g, carry):
            off = pl.multiple_of(g * KEY_TILE, KEY_TILE)
            key = skey_s[pl.ds(off, KEY_TILE), :]
            cut = (key == t) & ((off + row) > j_cut)
            skey_s[pl.ds(off, KEY_TILE), :] = jnp.where(cut, INT_MIN, key)
            return carry

        lax.fori_loop(0, n_att, drop, 0)

    t_sel = jnp.maximum(t, INT_MIN + 1)

    def mask_tile(g, carry):
        off = pl.multiple_of(g * KEY_TILE, KEY_TILE)
        mb_s[pl.ds(off, KEY_TILE), :] = jnp.where(
            skey_s[pl.ds(off, KEY_TILE), :] >= t_sel, 0.0, NEG_BIG)
        return carry

    lax.fori_loop(0, n_att, mask_tile, 0)

    def logits_tile(g, m8, table):
        off = pl.multiple_of(g * KEY_TILE, KEY_TILE)
        kp_t = kp_s[pl.ds(off, KEY_TILE), :]
        mb = mb_s[pl.ds(off, KEY_TILE), :]
        out = []
        for c in range(n_dots):
            lg = lax.dot_general(kp_t, q_s[c * dot_w:(c + 1) * dot_w, :], _NT_DIMS,
                                 preferred_element_type=F32)
            for s in range(HEADS_PER_DOT):
                h = c * HEADS_PER_DOT + s
                x = lg[:, s * Q_BLOCK:(s + 1) * Q_BLOCK] + mb
                if table is not None:
                    x = x + nb_ref[table, h]
                l_s[h, pl.ds(off, KEY_TILE), :] = x
                out.append(jnp.maximum(m8[h], _fold8(x, jnp.maximum)))
        return tuple(out)

    n_near = jnp.where((i & 1) == 0, jnp.minimum(n_att, 2), 1)
    n_far = n_att - n_near
    m8 = tuple(jnp.full((SUBLANES, Q_BLOCK), NEG_BIG, F32) for _ in range(n_heads))
    m8 = lax.fori_loop(0, n_far, lambda g, m: logits_tile(g, m, None), m8)
    m8 = lax.fori_loop(n_far, n_att, lambda g, m: logits_tile(g, m, 2 * g - i + 2), m8)
    m_row = [jnp.max(m, axis=0, keepdims=True) for m in m8]

    acc_s[...] = jnp.zeros(acc_s.shape, F32)

    def pv_tile(g, l8):
        off = pl.multiple_of(g * KEY_TILE, KEY_TILE)
        vt_t = vt_s[:, pl.ds(off, KEY_TILE)]
        out = []
        for c in range(n_dots):
            ps = []
            for s in range(HEADS_PER_DOT):
                h = c * HEADS_PER_DOT + s
                p = jnp.exp2(l_s[h, pl.ds(off, KEY_TILE), :] - m_row[h])
                out.append(l8[h] + _fold8(p, jnp.add))
                ps.append(p.astype(BF16))
            acc_s[c] += jnp.dot(vt_t, jnp.concatenate(ps, axis=1), preferred_element_type=F32)
        return tuple(out)

    l8 = tuple(jnp.zeros((SUBLANES, Q_BLOCK), F32) for _ in range(n_heads))
    l8 = lax.fori_loop(0, n_att, pv_tile, l8)

    for c in range(n_dots):
        acc = acc_s[c]
        for s in range(HEADS_PER_DOT):
            h = c * HEADS_PER_DOT + s
            denom = jnp.sum(l8[h], axis=0, keepdims=True)
            ot_s[h * HEAD_DIM:(h + 1) * HEAD_DIM, :] = acc[:, s * Q_BLOCK:(s + 1) * Q_BLOCK] / denom
    o_ref[...] = ot_s[...].T.astype(o_ref.dtype)


def _attention(zs, zb, near_bias, far_q, batch, seq, n_heads, qi_block):
    d_q = n_heads * HEAD_DIM
    nqb = seq // Q_BLOCK
    k_top = min(INDEX_TOPK, seq // 4)
    assert seq % COUNT_TILE == 0 and n_heads % HEADS_PER_DOT == 0
    kern = functools.partial(_attn_kernel, k_top=k_top, n_heads=n_heads,
                             pos_bits=max(1, (seq - 1).bit_length()))
    return pl.pallas_call(
        kern,
        grid=(batch, nqb),
        in_specs=[
            pl.BlockSpec((seq, 2 * LANES), lambda b, i: (b, 0)),
            pl.BlockSpec((Q_BLOCK, d_q), lambda b, i: (b * nqb + i, 0)),
            pl.BlockSpec((Q_BLOCK, IDX_HEADS * IDX_DIM), lambda b, i: (b * nqb + i, qi_block)),
            pl.BlockSpec(far_q.shape, lambda b, i: (0, 0)),
            pl.BlockSpec(near_bias.shape, lambda b, i: (0, 0, 0, 0)),
        ],
        out_specs=pl.BlockSpec((Q_BLOCK, d_q), lambda b, i: (b * nqb + i, 0)),
        out_shape=jax.ShapeDtypeStruct((batch * seq, d_q), BF16),
        scratch_shapes=[
            pltpu.VMEM((seq, LANES), BF16),
            pltpu.VMEM((seq, IDX_DIM), BF16),
            pltpu.VMEM((KV_DIM, seq), BF16),
            pltpu.VMEM((n_heads * Q_BLOCK, LANES), BF16),
            pltpu.VMEM((IDX_HEADS * Q_BLOCK, IDX_DIM), BF16),
            pltpu.VMEM((seq, Q_BLOCK), I32),
            pltpu.VMEM((seq, Q_BLOCK), I16),
            pltpu.VMEM((seq, Q_BLOCK), I16),
            pltpu.VMEM((seq, Q_BLOCK), F32),
            pltpu.VMEM((n_heads, seq, Q_BLOCK), F32),
            pltpu.VMEM((n_heads // HEADS_PER_DOT, KV_DIM, HEADS_PER_DOT * Q_BLOCK), F32),
            pltpu.VMEM((d_q, Q_BLOCK), F32),
        ],
        compiler_params=_params(2),
        name="dsa_attention",
    )(zs, zb, zb, far_q, near_bias)


def _t5_bucket(rel):
    half = REL_BUCKETS // 2
    max_exact = half // 2
    ret = jnp.where(rel > 0, half, 0)
    n = jnp.abs(rel)
    nf = jnp.maximum(n, 1).astype(F32)
    large = max_exact + (jnp.log(nf / max_exact) / math.log(REL_MAX_DIST / max_exact)
                         * (half - max_exact)).astype(I32)
    large = jnp.minimum(large, half - 1)
    return ret + jnp.where(n < max_exact, n, large)


def _bias_tables(rel_bias):
    n_heads = rel_bias.shape[1]
    far = rel_bias[_t5_bucket(jnp.asarray(-2 * REL_MAX_DIST, I32))].astype(F32) * LOG2E
    terms, rest = [], far
    for _ in range(N_BIAS_LANES):
        terms.append(rest.astype(BF16).astype(F32))
        rest = rest - terms[-1]
    far_q = jnp.zeros((n_heads, LANES - HEAD_DIM), F32).at[:, :N_BIAS_LANES].set(
        jnp.stack(terms, axis=1))
    far_sum = sum(terms[1:], terms[0])
    delta = (np.arange(3)[:, None, None] - 2) * Q_BLOCK
    rel = delta + np.arange(KEY_TILE)[None, :, None] - np.arange(Q_BLOCK)[None, None, :]
    near = rel_bias[_t5_bucket(jnp.asarray(rel, I32))].astype(F32) * LOG2E
    near = jnp.moveaxis(near, -1, 1) - far_sum[None, :, None, None]
    return near, far_q


def _neg_expm1(y):
    series = -y * (1.0 + y * (1.0 / 2 + y * (1.0 / 6 + y * (1.0 / 24 + y * (1.0 / 120)))))
    return jnp.where(y > -0.1, series, 1.0 - jnp.exp(y))


def _rglru_kernel(xr_ref, yr_ref, cw_ref, cb_ref, wa_ref, ba_ref, wx_ref, bx_ref, lam_ref,
                  o_ref, ext_s, a_s, u_s, h_s, *, ts, bw):
    j = pl.program_id(1)

    @pl.when(j == 0)
    def _():
        ext_s[0:SUBLANES, :] = jnp.zeros((SUBLANES, ext_s.shape[1]), F32)
        h_s[...] = jnp.zeros(h_s.shape, F32)

    x = xr_ref[...].astype(F32)
    ext_s[SUBLANES:SUBLANES + ts, :] = x
    xc = ext_s[SUBLANES - 3:SUBLANES - 3 + ts, :] * cw_ref[0:1, :]
    xc = xc + ext_s[SUBLANES - 2:SUBLANES - 2 + ts, :] * cw_ref[1:2, :]
    xc = xc + ext_s[SUBLANES - 1:SUBLANES - 1 + ts, :] * cw_ref[2:3, :]
    xc = xc + x * cw_ref[3:4, :]
    xc = xc + cb_ref[...]
    ext_s[0:SUBLANES, :] = ext_s[ts:ts + SUBLANES, :]

    sp = jax.nn.softplus(-lam_ref[...])
    for n in range(RNN_BLOCKS):
        cs = slice(n * bw, (n + 1) * bw)
        xb = xc[:, cs]
        xb16 = xb.astype(BF16)
        r = jax.nn.sigmoid(jnp.dot(xb16, wa_ref[n], preferred_element_type=F32) + ba_ref[:, cs])
        g = jax.nn.sigmoid(jnp.dot(xb16, wx_ref[n], preferred_element_type=F32) + bx_ref[:, cs])
        log_a = -LRU_C * r * sp[:, cs]
        a_s[:, cs] = jnp.exp(log_a)
        u_s[:, cs] = jnp.sqrt(_neg_expm1(2.0 * log_a)) * (g * xb)

    def step(t, h):
        h = a_s[pl.ds(t, 1), :] * h + u_s[pl.ds(t, 1), :]
        u_s[pl.ds(t, 1), :] = h
        return h

    h_s[...] = lax.fori_loop(0, ts, step, h_s[...], unroll=8)
    o_ref[...] = (u_s[...] * jax.nn.gelu(yr_ref[...].astype(F32))).astype(o_ref.dtype)


def _rglru(zb, conv_w, conv_b, wa, ba, wx, bx, lam, batch, seq, width, xr_block, yr_block, ts):
    nts = seq // ts
    bw = width // RNN_BLOCKS
    row = lambda v: v.reshape(1, width)
    full = lambda shape: pl.BlockSpec(shape, lambda b, j: (0,) * len(shape))
    return pl.pallas_call(
        functools.partial(_rglru_kernel, ts=ts, bw=bw),
        grid=(batch, nts),
        in_specs=[pl.BlockSpec((ts, width), lambda b, j: (b * nts + j, xr_block)),
                  pl.BlockSpec((ts, width), lambda b, j: (b * nts + j, yr_block)),
                  full((CONV_W, width)), full((1, width)),
                  full((RNN_BLOCKS, bw, bw)), full((1, width)),
                  full((RNN_BLOCKS, bw, bw)), full((1, width)), full((1, width))],
        out_specs=pl.BlockSpec((ts, width), lambda b, j: (b * nts + j, 0)),
        out_shape=jax.ShapeDtypeStruct((batch * seq, width), BF16),
        scratch_shapes=[pltpu.VMEM((ts + 2 * SUBLANES, width), F32),
                        pltpu.VMEM((ts, width), F32),
                        pltpu.VMEM((ts, width), F32),
                        pltpu.VMEM((1, width), F32)],
        compiler_params=_params(2),
        name="rglru",
    )(zb, zb, conv_w, row(conv_b), wa.astype(BF16), row(ba), wx.astype(BF16), row(bx), row(lam))


def _layer_norm(y, g, b):
    mu = jnp.mean(y, axis=-1, keepdims=True)
    var = jnp.mean(jnp.square(y - mu), axis=-1, keepdims=True)
    return (y - mu) * lax.rsqrt(var + LN_EPS) * g + b


def _merge_kernel(oa_ref, or_ref, ga_ref, gr_ref, x_ref, wba_ref, wbr_ref, wo_ref, g_ref, b_ref,
                  x1_ref, x1b_ref, *, alpha):
    a = jnp.dot(oa_ref[...], wba_ref[...], preferred_element_type=F32)
    r = jnp.dot(or_ref[...], wbr_ref[...], preferred_element_type=F32)
    merged = (jax.nn.sigmoid(ga_ref[...].astype(F32)) * a
              + jax.nn.sigmoid(gr_ref[...].astype(F32)) * r)
    mix = jnp.dot(merged.astype(BF16), wo_ref[...], preferred_element_type=F32)
    x1 = _layer_norm(alpha * x_ref[...] + mix, g_ref[...], b_ref[...])
    x1_ref[...] = x1
    x1b_ref[...] = x1.astype(BF16)


def _merge_ln(oa, orn, zb, x, wba, wbr, wo, g, b, alpha, ga_block, gr_block, tm):
    m, d = x.shape
    tok = lambda blk: pl.BlockSpec((tm, d), lambda i: (i, blk))
    full = lambda shape: pl.BlockSpec(shape, lambda i: (0,) * len(shape))
    return pl.pallas_call(
        functools.partial(_merge_kernel, alpha=alpha),
        grid=(m // tm,),
        in_specs=[tok(0), tok(0), tok(ga_block), tok(gr_block), tok(0),
                  full((d, d)), full((d, d)), full((d, d)), full((1, d)), full((1, d))],
        out_specs=[tok(0), tok(0)],
        out_shape=[jax.ShapeDtypeStruct((m, d), F32), jax.ShapeDtypeStruct((m, d), BF16)],
        compiler_params=_params(1),
        name="merge_ln1",
    )(oa, orn, zb, zb, x, wba, wbr, wo, g.reshape(1, d), b.reshape(1, d))


def _ple_kernel(*refs, alpha, routed):
    if routed:
        (x1_ref, x1b_ref, p_ref, pg_ref, pp_ref, g_ref, b_ref,
         ya_ref, yb_ref, gt_ref, x2_ref, x2b_ref) = refs
        f = (gt_ref[:, 0:1] * ya_ref[...].astype(F32)
             + gt_ref[:, 1:2] * yb_ref[...].astype(F32))
    else:
        x1_ref, x1b_ref, p_ref, pg_ref, pp_ref, g_ref, b_ref, f_ref, x2_ref, x2b_ref = refs
        f = f_ref[...].astype(F32)
    gate = jax.nn.sigmoid(jnp.dot(x1b_ref[...], pg_ref[...], preferred_element_type=F32))
    proj = jnp.dot(p_ref[...].astype(BF16), pp_ref[...], preferred_element_type=F32)
    x2 = _layer_norm(alpha * x1_ref[...] + f + gate * proj, g_ref[...], b_ref[...])
    x2_ref[...] = x2
    x2b_ref[...] = x2.astype(BF16)


def _ple_ln(x1, x1b, p_all, layer, pg, pp, g, b, f_parts, alpha, tm):
    m, d = x1.shape
    pd = p_all.shape[1]
    nt = m // tm
    tok = pl.BlockSpec((tm, d), lambda i: (i, 0))
    full = lambda shape: pl.BlockSpec(shape, lambda i: (0,) * len(shape))
    routed = len(f_parts) == 3
    f_specs = [tok, tok, pl.BlockSpec((tm, LANES), lambda i: (i, 0))] if routed else [tok]
    return pl.pallas_call(
        functools.partial(_ple_kernel, alpha=alpha, routed=routed),
        grid=(nt,),
        in_specs=[tok, tok, pl.BlockSpec((tm, pd), lambda i: (layer * nt + i, 0)),
                  full((d, d)), full((pd, d)), full((1, d)), full((1, d))] + f_specs,
        out_specs=[tok, tok],
        out_shape=[jax.ShapeDtypeStruct((m, d), F32), jax.ShapeDtypeStruct((m, d), BF16)],
        compiler_params=_params(1),
        name="ple_ln2",
    )(x1, x1b, p_all, pg, pp, g.reshape(1, d), b.reshape(1, d), *f_parts)


def _ffn_kernel(te_ref, nt_ref, x_ref, wg_ref, wu_ref, wd_ref, o_ref, acc_s):
    t = pl.program_id(0)
    j = pl.program_id(1)
    last = pl.num_programs(1) - 1
    live = t < nt_ref[0]

    @pl.when(live)
    def _():
        wg = wg_ref[0].astype(BF16)
        wu = wu_ref[0].astype(BF16)
        wd = wd_ref[0].astype(BF16)
        x = x_ref[...]
        gate = jnp.dot(x, wg, preferred_element_type=F32)
        up = jnp.dot(x, wu, preferred_element_type=F32)
        hid = (jax.nn.silu(gate) * up).astype(BF16)
        cw = wd.shape[1] // DOWN_SPLIT
        for c in range(DOWN_SPLIT):
            cols = slice(c * cw, (c + 1) * cw)
            part = jnp.dot(hid, wd[:, cols], preferred_element_type=F32)
            acc_s[:, cols] = jnp.where(j == 0, part, acc_s[:, cols] + part)

    @pl.when(j == last)
    def _():
        o_ref[...] = jnp.where(live, acc_s[...], 0.0).astype(o_ref.dtype)


def _ffn(x, tile_expert, n_live, wg, wu, wd, tm, tf):
    rows, d = x.shape
    f = wg.shape[2]
    fcol = lambda t, j, te, nt: jnp.where(t < nt[0], j, 0)
    grid_spec = pltpu.PrefetchScalarGridSpec(
        num_scalar_prefetch=2,
        grid=(rows // tm, f // tf),
        in_specs=[pl.BlockSpec((tm, d), lambda t, j, te, nt: (t, 0)),
                  pl.BlockSpec((1, d, tf), lambda t, j, te, nt: (te[t], 0, fcol(t, j, te, nt))),
                  pl.BlockSpec((1, d, tf), lambda t, j, te, nt: (te[t], 0, fcol(t, j, te, nt))),
                  pl.BlockSpec((1, tf, d), lambda t, j, te, nt: (te[t], fcol(t, j, te, nt), 0))],
        out_specs=pl.BlockSpec((tm, d), lambda t, j, te, nt: (t, 0)),
        scratch_shapes=[pltpu.VMEM((tm, d), F32)])
    return pl.pallas_call(
        _ffn_kernel, grid_spec=grid_spec,
        out_shape=jax.ShapeDtypeStruct((rows, d), BF16),
        compiler_params=_params(2),
        name="swiglu_ffn",
    )(tile_expert, n_live, x, wg, wu, wd)


def _router_kernel(x_ref, w_ref, b_ref, idx_ref, gate_ref):
    lg = jnp.dot(x_ref[...], w_ref[...], preferred_element_type=F32) + b_ref[...]
    lane = lax.broadcasted_iota(I32, lg.shape, 1)
    v1 = jnp.max(lg, axis=1, keepdims=True)
    i1 = jnp.min(jnp.where(lg == v1, lane, LANES), axis=1, keepdims=True)
    lg2 = jnp.where(lane == i1, -jnp.inf, lg)
    v2 = jnp.max(lg2, axis=1, keepdims=True)
    i2 = jnp.min(jnp.where(lg2 == v2, lane, LANES), axis=1, keepdims=True)
    e2 = jnp.exp(v2 - v1)
    denom = 1.0 + e2
    idx_ref[...] = jnp.where(lane == 0, i1, jnp.where(lane == 1, i2, 0))
    gate_ref[...] = jnp.where(lane == 0, 1.0 / denom, jnp.where(lane == 1, e2 / denom, 0.0))


def _router(x1, router, router_b, tm):
    m, d = x1.shape
    w = jnp.zeros((d, LANES), F32).at[:, :N_EXPERTS].set(router.astype(F32))
    b = jnp.full((1, LANES), -jnp.inf, F32).at[0, :N_EXPERTS].set(router_b.astype(F32))
    return pl.pallas_call(
        _router_kernel,
        grid=(m // tm,),
        in_specs=[pl.BlockSpec((tm, d), lambda i: (i, 0)),
                  pl.BlockSpec((d, LANES), lambda i: (0, 0)),
                  pl.BlockSpec((1, LANES), lambda i: (0, 0))],
        out_specs=[pl.BlockSpec((tm, LANES), lambda i: (i, 0)),
                   pl.BlockSpec((tm, LANES), lambda i: (i, 0))],
        out_shape=[jax.ShapeDtypeStruct((m, LANES), I32), jax.ShapeDtypeStruct((m, LANES), F32)],
        compiler_params=_params(1),
        name="router_top2",
    )(x1, w, b)


def _dispatch_plan(top_i, tm):
    m = top_i.shape[0]
    n_assign = m * TOP_K
    e_flat = top_i.reshape(n_assign)
    onehot = (e_flat[:, None] == jnp.arange(N_EXPERTS, dtype=I32)[None, :]).astype(I32)
    csum = jnp.cumsum(onehot, axis=0)
    counts = csum[-1]
    rank = jnp.sum(csum * onehot, axis=1) - 1
    padded = ((counts + tm - 1) // tm) * tm
    ends = jnp.cumsum(padded)
    pos = (ends - padded)[e_flat] + rank
    n_rows = n_assign + N_EXPERTS * tm
    src = jnp.zeros((n_rows,), I32).at[pos].set(jnp.arange(n_assign, dtype=I32) // TOP_K)
    tile_start = jnp.arange(n_rows // tm, dtype=I32) * tm
    tile_expert = jnp.minimum(jnp.searchsorted(ends, tile_start, side="right"),
                              N_EXPERTS - 1).astype(I32)
    n_live = (ends[-1] // tm).astype(I32).reshape(1)
    return src, pos.reshape(m, TOP_K), tile_expert, n_live


def _pick(n, candidates):
    for c in candidates:
        if n % c == 0:
            return c
    return n


def _forward(x, p, w_in, conv_w, conv_b, lru_wa, lru_ba, lru_wx, lru_bx, lru_lam, w_br_attn,
             w_br_rnn, w_o, rel_bias, ln1_g, ln1_b, ffn_w_gate, ffn_w_up, ffn_w_down, moe_router,
             moe_router_b, moe_w_gate, moe_w_up, moe_w_down, ple_w_gate, ple_w_proj, ln2_g, ln2_b):
    batch, seq, d = x.shape
    depth = w_in.shape[0]
    m = batch * seq
    n_heads = d // HEAD_DIM
    alpha = (2 * depth) ** 0.25
    d_q, d_qi = n_heads * HEAD_DIM, IDX_HEADS * IDX_DIM
    assert d_q == d and w_br_rnn.shape[1] == d and (d_q + 4 * d) % d_qi == 0

    sizes = (d_q, KV_DIM, KV_DIM, d_qi, IDX_DIM, IDX_HEADS, d, d, d, d)
    offs = np.concatenate([[0], np.cumsum(sizes)])
    col = lambda w, k: w[:, offs[k]:offs[k + 1]]
    q_scale = HEAD_DIM ** -0.5 * LOG2E
    big_scale = jnp.concatenate([jnp.full((1, d_q), q_scale, F32),
                                 jnp.ones((1, 4 * d + d_qi), F32)], axis=1)
    small_scale = jnp.ones((1, 2 * LANES), F32)
    xr_block, yr_block, ga_block, gr_block = 1, 2, 3, 4
    qi_block = (d_q + 4 * d) // d_qi

    near_bias, far_q = _bias_tables(rel_bias)

    tm = _pick(m, (1024, 512, 256, 128))
    tn_big = _pick(d_q + 4 * d + d_qi, (512, 256, 128))
    ts = _pick(seq, (256, 128))
    tm_res = _pick(m, (512, 256, 128))

    xf = x.reshape(m, d)
    xb = xf.astype(BF16)
    p_all = p.reshape(depth * m, p.shape[-1])

    for i in range(depth):
        w = w_in[i]
        w_big = jnp.concatenate([col(w, 0), col(w, 6), col(w, 7), col(w, 8), col(w, 9), col(w, 3)],
                                axis=1).astype(BF16)
        w_small = jnp.concatenate(
            [col(w, 1), col(w, 2), col(w, 4), col(w, 5),
             jnp.zeros((d, 2 * LANES - 2 * KV_DIM - IDX_DIM - IDX_HEADS), w.dtype)], axis=1).astype(BF16)
        zb = _matmul(xb, w_big, big_scale, BF16, _pick(m, (2 * tm, tm)), tn_big)
        zs = _matmul(xb, w_small, small_scale, F32, tm, 2 * LANES)

        o_attn = _attention(zs, zb, near_bias, far_q, batch, seq, n_heads, qi_block)
        o_rnn = _rglru(zb, conv_w[i], conv_b[i], lru_wa[i], lru_ba[i], lru_wx[i], lru_bx[i],
                       lru_lam[i], batch, seq, d, xr_block, yr_block, ts)
        x1, x1b = _merge_ln(o_attn, o_rnn, zb, xf, w_br_attn[i].astype(BF16),
                            w_br_rnn[i].astype(BF16), w_o[i].astype(BF16), ln1_g[i], ln1_b[i],
                            alpha, ga_block, gr_block, tm_res)

        j = i // 2
        if i % 2 == 0:
            f_dim = ffn_w_gate.shape[-1]
            tf = _pick(f_dim, (512, f_dim // 2))
            n_tiles = m // tm
            f_parts = [_ffn(x1b, jnp.zeros((n_tiles,), I32), jnp.full((1,), n_tiles, I32),
                            ffn_w_gate[j][None].astype(BF16), ffn_w_up[j][None].astype(BF16),
                            ffn_w_down[j][None].astype(BF16), tm, tf)]
        else:
            f_dim = moe_w_gate.shape[-1]
            tf = _pick(f_dim, (512, f_dim // 2))
            idx, gates = _router(x1, moe_router[j], moe_router_b[j], tm_res)
            src, pos, tile_expert, n_live = _dispatch_plan(idx[:, :TOP_K], tm)
            all_experts = lambda w: w.reshape((-1,) + w.shape[2:])
            y = _ffn(x1b[src], tile_expert + j * N_EXPERTS, n_live, all_experts(moe_w_gate),
                     all_experts(moe_w_up), all_experts(moe_w_down), tm, tf)
            f_parts = [y[pos[:, 0]], y[pos[:, 1]], gates]

        xf, xb = _ple_ln(x1, x1b, p_all, i, ple_w_gate[i].astype(BF16), ple_w_proj[i].astype(BF16),
                         ln2_g[i], ln2_b[i], f_parts, alpha, tm_res)

    return xf.reshape(batch, seq, d)


@jax.jit
def kernel(x, p, w_in, conv_w, conv_b, lru_wa, lru_ba, lru_wx, lru_bx, lru_lam, w_br_attn, w_br_rnn,
           w_o, rel_bias, ln1_g, ln1_b, ffn_w_gate, ffn_w_up, ffn_w_down, moe_router, moe_router_b,
           moe_w_gate, moe_w_up, moe_w_down, ple_w_gate, ple_w_proj, ln2_g, ln2_b):
    return _forward(x, p, w_in, conv_w, conv_b, lru_wa, lru_ba, lru_wx, lru_bx, lru_lam, w_br_attn,
                    w_br_rnn, w_o, rel_bias, ln1_g, ln1_b, ffn_w_gate, ffn_w_up, ffn_w_down,
                    moe_router, moe_router_b, moe_w_gate, moe_w_up, moe_w_down, ple_w_gate,
                    ple_w_proj, ln2_g, ln2_b)
```

```python
import functools
import math

import jax
import jax.numpy as jnp
import numpy as np
from jax import lax
from jax.experimental import pallas as pl
from jax.experimental.pallas import tpu as pltpu

CHUNK = 64
CHUNK_SHIFT = CHUNK.bit_length() - 1
Q_BLOCK = 128
HEAD_DIM = 64
KV_DIM = 64
IDX_HEADS = 8
IDX_DIM = 64
INDEX_TOPK = 256
RNN_BLOCKS = 8
CONV_W = 4
LRU_C = 8.0
REL_BUCKETS = 32
REL_MAX_DIST = 128
N_EXPERTS = 8
TOP_K = 2
LN_EPS = 1e-5

LANES = 128
SUBLANES = 8
VMEM_LIMIT_BYTES = 56 * 1024 * 1024

KEY_TILE = 2 * Q_BLOCK
COUNT_TILE = 2 * KEY_TILE
HEADS_PER_DOT = 2
N_BIAS_LANES = 3
DOWN_SPLIT = 4
INT_MIN = -(2 ** 31)
NEG_BIG = -1e30
LOG2E = math.log2(math.e)

BF16 = jnp.bfloat16
F32 = jnp.float32
I32 = jnp.int32

_NT_DIMS = (((1,), (1,)), ((), ()))


def _params(n_axes):
    return pltpu.CompilerParams(
        dimension_semantics=("arbitrary",) * n_axes,
        vmem_limit_bytes=VMEM_LIMIT_BYTES)


def _mm_kernel(x_ref, w_ref, s_ref, o_ref):
    acc = jnp.dot(x_ref[...], w_ref[...], preferred_element_type=F32)
    o_ref[...] = (acc * s_ref[...]).astype(o_ref.dtype)


def _matmul(x, w, scale, out_dtype, tm, tn):
    m, k = x.shape
    n = w.shape[1]
    return pl.pallas_call(
        _mm_kernel,
        grid=(m // tm, n // tn),
        in_specs=[pl.BlockSpec((tm, k), lambda i, j: (i, 0)),
                  pl.BlockSpec((k, tn), lambda i, j: (0, j)),
                  pl.BlockSpec((1, tn), lambda i, j: (0, j))],
        out_specs=pl.BlockSpec((tm, tn), lambda i, j: (i, j)),
        out_shape=jax.ShapeDtypeStruct((m, n), out_dtype),
        compiler_params=_params(2),
        name="proj_in",
    )(x, w, scale)


def _fold_rows(x, op, rows):
    parts = [x[r * rows:(r + 1) * rows, :] for r in range(x.shape[0] // rows)]
    while len(parts) > 1:
        nxt = [op(parts[j], parts[j + 1]) for j in range(0, len(parts) - 1, 2)]
        if len(parts) % 2:
            nxt.append(parts[-1])
        parts = nxt
    return parts[0]


def _fold8(x, op):
    return _fold_rows(x, op, SUBLANES)


def _attn_kernel(zs_ref, q_ref, qi_ref, fbq_ref, nb_ref, o_ref,
                 kp_s, ki_s, vt_s, q_s, qi_s, skey_s, mb_s, l_s, acc_s, ot_s,
                 *, k_top, n_heads, pos_bits):
    b = pl.program_id(0)
    i = pl.program_id(1)
    seq = zs_ref.shape[0]
    n_dots = n_heads // HEADS_PER_DOT
    dot_w = HEADS_PER_DOT * Q_BLOCK
    start = pl.multiple_of(i * Q_BLOCK, Q_BLOCK)
    n_att = (i + 2) // 2
    n_cnt = (n_att + 1) // 2

    @pl.when((b == 0) & (i == 0))
    def _():
        for h in range(n_heads):
            q_s[h * Q_BLOCK:(h + 1) * Q_BLOCK, HEAD_DIM:] = jnp.broadcast_to(
                fbq_ref[h:h + 1, :], (Q_BLOCK, LANES - HEAD_DIM)).astype(BF16)

    @pl.when(i == 0)
    def _():
        kv = zs_ref[:, 0:2 * KV_DIM]
        col = lax.broadcasted_iota(I32, kv.shape, 1)
        ones = jnp.where(col < KV_DIM + N_BIAS_LANES, 1.0, 0.0)
        kp_s[...] = jnp.where(col < KV_DIM, kv, ones).astype(BF16)
        vt_s[...] = kv.T[KV_DIM:2 * KV_DIM, :].astype(BF16)
        ki_s[...] = zs_ref[:, 2 * KV_DIM:2 * KV_DIM + IDX_DIM].astype(BF16)
        skey_s[...] = jnp.full(skey_s.shape, INT_MIN, I32)

    for h in range(n_heads):
        q_s[h * Q_BLOCK:(h + 1) * Q_BLOCK, 0:HEAD_DIM] = q_ref[:, h * HEAD_DIM:(h + 1) * HEAD_DIM]
    for h in range(IDX_HEADS):
        qi_s[h * Q_BLOCK:(h + 1) * Q_BLOCK, :] = qi_ref[:, h * IDX_DIM:(h + 1) * IDX_DIM]

    w_t = zs_ref[pl.ds(start, Q_BLOCK), LANES:2 * LANES].T

    row = lax.broadcasted_iota(I32, (KEY_TILE, Q_BLOCK), 0)
    lane = lax.broadcasted_iota(I32, (KEY_TILE, Q_BLOCK), 1)
    row_cnt = lax.broadcasted_iota(I32, (COUNT_TILE, Q_BLOCK), 0)
    q_chunk = lax.shift_right_logical(start + lane, CHUNK_SHIFT)

    def score_tile(g, carry):
        off = pl.multiple_of(g * KEY_TILE, KEY_TILE)
        ki_t = ki_s[pl.ds(off, KEY_TILE), :]
        acc = jnp.zeros((KEY_TILE, Q_BLOCK), F32)
        for c in range(IDX_HEADS // HEADS_PER_DOT):
            d = lax.dot_general(ki_t, qi_s[c * dot_w:(c + 1) * dot_w, :], _NT_DIMS,
                                preferred_element_type=F32)
            for s in range(HEADS_PER_DOT):
                h = c * HEADS_PER_DOT + s
                acc = acc + (jnp.maximum(d[:, s * Q_BLOCK:(s + 1) * Q_BLOCK], 0.0)
                             * w_t[IDX_DIM + h:IDX_DIM + h + 1, :])
        bits = lax.bitcast_convert_type(acc, I32)
        key = bits ^ ((bits >> 31) & jnp.int32(0x7FFFFFFF))
        k_chunk = lax.shift_right_logical(off + row, CHUNK_SHIFT)
        skey_s[pl.ds(off, KEY_TILE), :] = jnp.where(k_chunk <= q_chunk, key, INT_MIN)
        return carry

    lax.fori_loop(0, n_att, score_tile, 0)

    def count(pred):
        def body(g, c):
            off = pl.multiple_of(g * COUNT_TILE, COUNT_TILE)
            hit = pred(skey_s[pl.ds(off, COUNT_TILE), :], off + row_cnt)
            return c + _fold8(jnp.where(hit, 1, 0).astype(I32), jnp.add)
        c = lax.fori_loop(0, n_cnt, body, jnp.zeros((SUBLANES, Q_BLOCK), I32))
        return jnp.sum(c, axis=0, keepdims=True)

    def bisect(it, t):
        cand = t ^ lax.shift_left(jnp.int32(1), 31 - it)
        c = count(lambda key, pos: key >= cand)
        return jnp.where(c >= k_top, cand, t)

    t = lax.fori_loop(0, 32, bisect, jnp.full((1, Q_BLOCK), INT_MIN, I32))

    n_ge = count(lambda key, pos: key >= t)

    @pl.when(jnp.max(n_ge) > k_top)
    def _():
        n_gt = count(lambda key, pos: key > t)
        r_m1 = k_top - n_gt - 1

        def pos_search(it, j):
            cand = j | lax.shift_left(jnp.int32(1), pos_bits - 1 - it)
            c = count(lambda key, pos: (key == t) & (pos < cand))
            return jnp.where(c <= r_m1, cand, j)

        j_cut = lax.fori_loop(0, pos_bits, pos_search, jnp.zeros((1, Q_BLOCK), I32))

        def drop(g, carry):
            off = pl.multiple_of(g * KEY_TILE, KEY_TILE)
            key = skey_s[pl.ds(off, KEY_TILE), :]
            cut = (key == t) & ((off + row) > j_cut)
            skey_s[pl.ds(off, KEY_TILE), :] = jnp.where(cut, INT_MIN, key)
            return carry

        lax.fori_loop(0, n_att, drop, 0)

    t_sel = jnp.maximum(t, INT_MIN + 1)

    def mask_tile(g, carry):
        off = pl.multiple_of(g * KEY_TILE, KEY_TILE)
        mb_s[pl.ds(off, KEY_TILE), :] = jnp.where(
            skey_s[pl.ds(off, KEY_TILE), :] >= t_sel, 0.0, NEG_BIG)
        return carry

    lax.fori_loop(0, n_att, mask_tile, 0)

    def logits_tile(g, m8, table):
        off = pl.multiple_of(g * KEY_TILE, KEY_TILE)
        kp_t = kp_s[pl.ds(off, KEY_TILE), :]
        mb = mb_s[pl.ds(off, KEY_TILE), :]
        out = []
        for c in range(n_dots):
            lg = lax.dot_general(kp_t, q_s[c * dot_w:(c + 1) * dot_w, :], _NT_DIMS,
                                 preferred_element_type=F32)
            for s in range(HEADS_PER_DOT):
                h = c * HEADS_PER_DOT + s
                x = lg[:, s * Q_BLOCK:(s + 1) * Q_BLOCK] + mb
                if table is not None:
                    x = x + nb_ref[table, h]
                l_s[h, pl.ds(off, KEY_TILE), :] = x
                out.append(jnp.maximum(m8[h], _fold8(x, jnp.maximum)))
        return tuple(out)

    n_near = jnp.where((i & 1) == 0, jnp.minimum(n_att, 2), 1)
    n_far = n_att - n_near
    m8 = tuple(jnp.full((SUBLANES, Q_BLOCK), NEG_BIG, F32) for _ in range(n_heads))
    m8 = lax.fori_loop(0, n_far, lambda g, m: logits_tile(g, m, None), m8)
    m8 = lax.fori_loop(n_far, n_att, lambda g, m: logits_tile(g, m, 2 * g - i + 2), m8)
    m_row = [jnp.max(m, axis=0, keepdims=True) for m in m8]

    acc_s[...] = jnp.zeros(acc_s.shape, F32)

    def pv_tile(g, l8):
        off = pl.multiple_of(g * KEY_TILE, KEY_TILE)
        vt_t = vt_s[:, pl.ds(off, KEY_TILE)]
        out = []
        for c in range(n_dots):
            ps = []
            for s in range(HEADS_PER_DOT):
                h = c * HEADS_PER_DOT + s
                p = jnp.exp2(l_s[h, pl.ds(off, KEY_TILE), :] - m_row[h])
                out.append(l8[h] + _fold8(p, jnp.add))
                ps.append(p.astype(BF16))
            acc_s[c] += jnp.dot(vt_t, jnp.concatenate(ps, axis=1), preferred_element_type=F32)
        return tuple(out)

    l8 = tuple(jnp.zeros((SUBLANES, Q_BLOCK), F32) for _ in range(n_heads))
    l8 = lax.fori_loop(0, n_att, pv_tile, l8)

    for c in range(n_dots):
        acc = acc_s[c]
        for s in range(HEADS_PER_DOT):
            h = c * HEADS_PER_DOT + s
            denom = jnp.sum(l8[h], axis=0, keepdims=True)
            ot_s[h * HEAD_DIM:(h + 1) * HEAD_DIM, :] = acc[:, s * Q_BLOCK:(s + 1) * Q_BLOCK] / denom
    o_ref[...] = ot_s[...].T.astype(o_ref.dtype)


def _attention(zs, zb, near_bias, far_q, batch, seq, n_heads, qi_block):
    d_q = n_heads * HEAD_DIM
    nqb = seq // Q_BLOCK
    k_top = min(INDEX_TOPK, seq // 4)
    assert seq % COUNT_TILE == 0 and n_heads % HEADS_PER_DOT == 0
    kern = functools.partial(_attn_kernel, k_top=k_top, n_heads=n_heads,
                             pos_bits=max(1, (seq - 1).bit_length()))
    return pl.pallas_call(
        kern,
        grid=(batch, nqb),
        in_specs=[
            pl.BlockSpec((seq, 2 * LANES), lambda b, i: (b, 0)),
            pl.BlockSpec((Q_BLOCK, d_q), lambda b, i: (b * nqb + i, 0)),
            pl.BlockSpec((Q_BLOCK, IDX_HEADS * IDX_DIM), lambda b, i: (b * nqb + i, qi_block)),
            pl.BlockSpec(far_q.shape, lambda b, i: (0, 0)),
            pl.BlockSpec(near_bias.shape, lambda b, i: (0, 0, 0, 0)),
        ],
        out_specs=pl.BlockSpec((Q_BLOCK, d_q), lambda b, i: (b * nqb + i, 0)),
        out_shape=jax.ShapeDtypeStruct((batch * seq, d_q), BF16),
        scratch_shapes=[
            pltpu.VMEM((seq, LANES), BF16),
            pltpu.VMEM((seq, IDX_DIM), BF16),
            pltpu.VMEM((KV_DIM, seq), BF16),
            pltpu.VMEM((n_heads * Q_BLOCK, LANES), BF16),
            pltpu.VMEM((IDX_HEADS * Q_BLOCK, IDX_DIM), BF16),
            pltpu.VMEM((seq, Q_BLOCK), I32),
            pltpu.VMEM((seq, Q_BLOCK), F32),
            pltpu.VMEM((n_heads, seq, Q_BLOCK), F32),
            pltpu.VMEM((n_heads // HEADS_PER_DOT, KV_DIM, HEADS_PER_DOT * Q_BLOCK), F32),
            pltpu.VMEM((d_q, Q_BLOCK), F32),
        ],
        compiler_params=_params(2),
        name="dsa_attention",
    )(zs, zb, zb, far_q, near_bias)


def _t5_bucket(rel):
    half = REL_BUCKETS // 2
    max_exact = half // 2
    ret = jnp.where(rel > 0, half, 0)
    n = jnp.abs(rel)
    nf = jnp.maximum(n, 1).astype(F32)
    large = max_exact + (jnp.log(nf / max_exact) / math.log(REL_MAX_DIST / max_exact)
                         * (half - max_exact)).astype(I32)
    large = jnp.minimum(large, half - 1)
    return ret + jnp.where(n < max_exact, n, large)


def _bias_tables(rel_bias):
    n_heads = rel_bias.shape[1]
    far = rel_bias[_t5_bucket(jnp.asarray(-2 * REL_MAX_DIST, I32))].astype(F32) * LOG2E
    terms, rest = [], far
    for _ in range(N_BIAS_LANES):
        terms.append(rest.astype(BF16).astype(F32))
        rest = rest - terms[-1]
    far_q = jnp.zeros((n_heads, LANES - HEAD_DIM), F32).at[:, :N_BIAS_LANES].set(
        jnp.stack(terms, axis=1))
    far_sum = sum(terms[1:], terms[0])
    delta = (np.arange(3)[:, None, None] - 2) * Q_BLOCK
    rel = delta + np.arange(KEY_TILE)[None, :, None] - np.arange(Q_BLOCK)[None, None, :]
    near = rel_bias[_t5_bucket(jnp.asarray(rel, I32))].astype(F32) * LOG2E
    near = jnp.moveaxis(near, -1, 1) - far_sum[None, :, None, None]
    return near, far_q


def _neg_expm1(y):
    series = -y * (1.0 + y * (1.0 / 2 + y * (1.0 / 6 + y * (1.0 / 24 + y * (1.0 / 120)))))
    return jnp.where(y > -0.1, series, 1.0 - jnp.exp(y))


def _rglru_kernel(xr_ref, yr_ref, cw_ref, cb_ref, wa_ref, ba_ref, wx_ref, bx_ref, lam_ref,
                  o_ref, ext_s, a_s, u_s, h_s, *, ts, bw):
    j = pl.program_id(1)

    @pl.when(j == 0)
    def _():
        ext_s[0:SUBLANES, :] = jnp.zeros((SUBLANES, ext_s.shape[1]), F32)
        h_s[...] = jnp.zeros(h_s.shape, F32)

    x = xr_ref[...].astype(F32)
    ext_s[SUBLANES:SUBLANES + ts, :] = x
    xc = ext_s[SUBLANES - 3:SUBLANES - 3 + ts, :] * cw_ref[0:1, :]
    xc = xc + ext_s[SUBLANES - 2:SUBLANES - 2 + ts, :] * cw_ref[1:2, :]
    xc = xc + ext_s[SUBLANES - 1:SUBLANES - 1 + ts, :] * cw_ref[2:3, :]
    xc = xc + x * cw_ref[3:4, :]
    xc = xc + cb_ref[...]
    ext_s[0:SUBLANES, :] = ext_s[ts:ts + SUBLANES, :]

    sp = jax.nn.softplus(-lam_ref[...])
    for n in range(RNN_BLOCKS):
        cs = slice(n * bw, (n + 1) * bw)
        xb = xc[:, cs]
        xb16 = xb.astype(BF16)
        r = jax.nn.sigmoid(jnp.dot(xb16, wa_ref[n], preferred_element_type=F32) + ba_ref[:, cs])
        g = jax.nn.sigmoid(jnp.dot(xb16, wx_ref[n], preferred_element_type=F32) + bx_ref[:, cs])
        log_a = -LRU_C * r * sp[:, cs]
        a_s[:, cs] = jnp.exp(log_a)
        u_s[:, cs] = jnp.sqrt(_neg_expm1(2.0 * log_a)) * (g * xb)

    def step(t, h):
        h = a_s[pl.ds(t, 1), :] * h + u_s[pl.ds(t, 1), :]
        u_s[pl.ds(t, 1), :] = h
        return h

    h_s[...] = lax.fori_loop(0, ts, step, h_s[...], unroll=8)
    o_ref[...] = (u_s[...] * jax.nn.gelu(yr_ref[...].astype(F32))).astype(o_ref.dtype)


def _rglru(zb, conv_w, conv_b, wa, ba, wx, bx, lam, batch, seq, width, xr_block, yr_block, ts):
    nts = seq // ts
    bw = width // RNN_BLOCKS
    row = lambda v: v.reshape(1, width)
    full = lambda shape: pl.BlockSpec(shape, lambda b, j: (0,) * len(shape))
    return pl.pallas_call(
        functools.partial(_rglru_kernel, ts=ts, bw=bw),
        grid=(batch, nts),
        in_specs=[pl.BlockSpec((ts, width), lambda b, j: (b * nts + j, xr_block)),
                  pl.BlockSpec((ts, width), lambda b, j: (b * nts + j, yr_block)),
                  full((CONV_W, width)), full((1, width)),
                  full((RNN_BLOCKS, bw, bw)), full((1, width)),
                  full((RNN_BLOCKS, bw, bw)), full((1, width)), full((1, width))],
        out_specs=pl.BlockSpec((ts, width), lambda b, j: (b * nts + j, 0)),
        out_shape=jax.ShapeDtypeStruct((batch * seq, width), BF16),
        scratch_shapes=[pltpu.VMEM((ts + 2 * SUBLANES, width), F32),
                        pltpu.VMEM((ts, width), F32),
                        pltpu.VMEM((ts, width), F32),
                        pltpu.VMEM((1, width), F32)],
        compiler_params=_params(2),
        name="rglru",
    )(zb, zb, conv_w, row(conv_b), wa.astype(BF16), row(ba), wx.astype(BF16), row(bx), row(lam))


def _layer_norm(y, g, b):
    mu = jnp.mean(y, axis=-1, keepdims=True)
    var = jnp.mean(jnp.square(y - mu), axis=-1, keepdims=True)
    return (y - mu) * lax.rsqrt(var + LN_EPS) * g + b


def _merge_kernel(oa_ref, or_ref, ga_ref, gr_ref, x_ref, wba_ref, wbr_ref, wo_ref, g_ref, b_ref,
                  x1_ref, x1b_ref, *, alpha):
    a = jnp.dot(oa_ref[...], wba_ref[...], preferred_element_type=F32)
    r = jnp.dot(or_ref[...], wbr_ref[...], preferred_element_type=F32)
    merged = (jax.nn.sigmoid(ga_ref[...].astype(F32)) * a
              + jax.nn.sigmoid(gr_ref[...].astype(F32)) * r)
    mix = jnp.dot(merged.astype(BF16), wo_ref[...], preferred_element_type=F32)
    x1 = _layer_norm(alpha * x_ref[...] + mix, g_ref[...], b_ref[...])
    x1_ref[...] = x1
    x1b_ref[...] = x1.astype(BF16)


def _merge_ln(oa, orn, zb, x, wba, wbr, wo, g, b, alpha, ga_block, gr_block, tm):
    m, d = x.shape
    tok = lambda blk: pl.BlockSpec((tm, d), lambda i: (i, blk))
    full = lambda shape: pl.BlockSpec(shape, lambda i: (0,) * len(shape))
    return pl.pallas_call(
        functools.partial(_merge_kernel, alpha=alpha),
        grid=(m // tm,),
        in_specs=[tok(0), tok(0), tok(ga_block), tok(gr_block), tok(0),
                  full((d, d)), full((d, d)), full((d, d)), full((1, d)), full((1, d))],
        out_specs=[tok(0), tok(0)],
        out_shape=[jax.ShapeDtypeStruct((m, d), F32), jax.ShapeDtypeStruct((m, d), BF16)],
        compiler_params=_params(1),
        name="merge_ln1",
    )(oa, orn, zb, zb, x, wba, wbr, wo, g.reshape(1, d), b.reshape(1, d))


def _ple_kernel(*refs, alpha, routed):
    if routed:
        (x1_ref, x1b_ref, p_ref, pg_ref, pp_ref, g_ref, b_ref,
         ya_ref, yb_ref, gt_ref, x2_ref, x2b_ref) = refs
        f = (gt_ref[:, 0:1] * ya_ref[...].astype(F32)
             + gt_ref[:, 1:2] * yb_ref[...].astype(F32))
    else:
        x1_ref, x1b_ref, p_ref, pg_ref, pp_ref, g_ref, b_ref, f_ref, x2_ref, x2b_ref = refs
        f = f_ref[...].astype(F32)
    gate = jax.nn.sigmoid(jnp.dot(x1b_ref[...], pg_ref[...], preferred_element_type=F32))
    proj = jnp.dot(p_ref[...].astype(BF16), pp_ref[...], preferred_element_type=F32)
    x2 = _layer_norm(alpha * x1_ref[...] + f + gate * proj, g_ref[...], b_ref[...])
    x2_ref[...] = x2
    x2b_ref[...] = x2.astype(BF16)


def _ple_ln(x1, x1b, p_all, layer, pg, pp, g, b, f_parts, alpha, tm):
    m, d = x1.shape
    pd = p_all.shape[1]
    nt = m // tm
    tok = pl.BlockSpec((tm, d), lambda i: (i, 0))
    full = lambda shape: pl.BlockSpec(shape, lambda i: (0,) * len(shape))
    routed = len(f_parts) == 3
    f_specs = [tok, tok, pl.BlockSpec((tm, LANES), lambda i: (i, 0))] if routed else [tok]
    return pl.pallas_call(
        functools.partial(_ple_kernel, alpha=alpha, routed=routed),
        grid=(nt,),
        in_specs=[tok, tok, pl.BlockSpec((tm, pd), lambda i: (layer * nt + i, 0)),
                  full((d, d)), full((pd, d)), full((1, d)), full((1, d))] + f_specs,
        out_specs=[tok, tok],
        out_shape=[jax.ShapeDtypeStruct((m, d), F32), jax.ShapeDtypeStruct((m, d), BF16)],
        compiler_params=_params(1),
        name="ple_ln2",
    )(x1, x1b, p_all, pg, pp, g.reshape(1, d), b.reshape(1, d), *f_parts)


def _ffn_kernel(te_ref, nt_ref, x_ref, wg_ref, wu_ref, wd_ref, o_ref, acc_s):
    t = pl.program_id(0)
    j = pl.program_id(1)
    last = pl.num_programs(1) - 1
    live = t < nt_ref[0]

    @pl.when(live)
    def _():
        wg = wg_ref[0].astype(BF16)
        wu = wu_ref[0].astype(BF16)
        wd = wd_ref[0].astype(BF16)
        x = x_ref[...]
        gate = jnp.dot(x, wg, preferred_element_type=F32)
        up = jnp.dot(x, wu, preferred_element_type=F32)
        hid = (jax.nn.silu(gate) * up).astype(BF16)
        cw = wd.shape[1] // DOWN_SPLIT
        for c in range(DOWN_SPLIT):
            cols = slice(c * cw, (c + 1) * cw)
            part = jnp.dot(hid, wd[:, cols], preferred_element_type=F32)
            acc_s[:, cols] = jnp.where(j == 0, part, acc_s[:, cols] + part)

    @pl.when(j == last)
    def _():
        o_ref[...] = jnp.where(live, acc_s[...], 0.0).astype(o_ref.dtype)


def _ffn(x, tile_expert, n_live, wg, wu, wd, tm, tf):
    rows, d = x.shape
    f = wg.shape[2]
    fcol = lambda t, j, te, nt: jnp.where(t < nt[0], j, 0)
    grid_spec = pltpu.PrefetchScalarGridSpec(
        num_scalar_prefetch=2,
        grid=(rows // tm, f // tf),
        in_specs=[pl.BlockSpec((tm, d), lambda t, j, te, nt: (t, 0)),
                  pl.BlockSpec((1, d, tf), lambda t, j, te, nt: (te[t], 0, fcol(t, j, te, nt))),
                  pl.BlockSpec((1, d, tf), lambda t, j, te, nt: (te[t], 0, fcol(t, j, te, nt))),
                  pl.BlockSpec((1, tf, d), lambda t, j, te, nt: (te[t], fcol(t, j, te, nt), 0))],
        out_specs=pl.BlockSpec((tm, d), lambda t, j, te, nt: (t, 0)),
        scratch_shapes=[pltpu.VMEM((tm, d), F32)])
    return pl.pallas_call(
        _ffn_kernel, grid_spec=grid_spec,
        out_shape=jax.ShapeDtypeStruct((rows, d), BF16),
        compiler_params=_params(2),
        name="swiglu_ffn",
    )(tile_expert, n_live, x, wg, wu, wd)


def _router_kernel(x_ref, w_ref, b_ref, idx_ref, gate_ref):
    lg = jnp.dot(x_ref[...], w_ref[...], preferred_element_type=F32) + b_ref[...]
    lane = lax.broadcasted_iota(I32, lg.shape, 1)
    v1 = jnp.max(lg, axis=1, keepdims=True)
    i1 = jnp.min(jnp.where(lg == v1, lane, LANES), axis=1, keepdims=True)
    lg2 = jnp.where(lane == i1, -jnp.inf, lg)
    v2 = jnp.max(lg2, axis=1, keepdims=True)
    i2 = jnp.min(jnp.where(lg2 == v2, lane, LANES), axis=1, keepdims=True)
    e2 = jnp.exp(v2 - v1)
    denom = 1.0 + e2
    idx_ref[...] = jnp.where(lane == 0, i1, jnp.where(lane == 1, i2, 0))
    gate_ref[...] = jnp.where(lane == 0, 1.0 / denom, jnp.where(lane == 1, e2 / denom, 0.0))


def _router(x1, router, router_b, tm):
    m, d = x1.shape
    w = jnp.zeros((d, LANES), F32).at[:, :N_EXPERTS].set(router.astype(F32))
    b = jnp.full((1, LANES), -jnp.inf, F32).at[0, :N_EXPERTS].set(router_b.astype(F32))
    return pl.pallas_call(
        _router_kernel,
        grid=(m // tm,),
        in_specs=[pl.BlockSpec((tm, d), lambda i: (i, 0)),
                  pl.BlockSpec((d, LANES), lambda i: (0, 0)),
                  pl.BlockSpec((1, LANES), lambda i: (0, 0))],
        out_specs=[pl.BlockSpec((tm, LANES), lambda i: (i, 0)),
                   pl.BlockSpec((tm, LANES), lambda i: (i, 0))],
        out_shape=[jax.ShapeDtypeStruct((m, LANES), I32), jax.ShapeDtypeStruct((m, LANES), F32)],
        compiler_params=_params(1),
        name="router_top2",
    )(x1, w, b)


def _dispatch_plan(top_i, tm):
    m = top_i.shape[0]
    n_assign = m * TOP_K
    e_flat = top_i.reshape(n_assign)
    onehot = (e_flat[:, None] == jnp.arange(N_EXPERTS, dtype=I32)[None, :]).astype(I32)
    csum = jnp.cumsum(onehot, axis=0)
    counts = csum[-1]
    rank = jnp.sum(csum * onehot, axis=1) - 1
    padded = ((counts + tm - 1) // tm) * tm
    ends = jnp.cumsum(padded)
    pos = (ends - padded)[e_flat] + rank
    n_rows = n_assign + N_EXPERTS * tm
    src = jnp.zeros((n_rows,), I32).at[pos].set(jnp.arange(n_assign, dtype=I32) // TOP_K)
    tile_start = jnp.arange(n_rows // tm, dtype=I32) * tm
    tile_expert = jnp.minimum(jnp.searchsorted(ends, tile_start, side="right"),
                              N_EXPERTS - 1).astype(I32)
    n_live = (ends[-1] // tm).astype(I32).reshape(1)
    return src, pos.reshape(m, TOP_K), tile_expert, n_live


def _pick(n, candidates):
    for c in candidates:
        if n % c == 0:
            return c
    return n


def _forward(x, p, w_in, conv_w, conv_b, lru_wa, lru_ba, lru_wx, lru_bx, lru_lam, w_br_attn,
             w_br_rnn, w_o, rel_bias, ln1_g, ln1_b, ffn_w_gate, ffn_w_up, ffn_w_down, moe_router,
             moe_router_b, moe_w_gate, moe_w_up, moe_w_down, ple_w_gate, ple_w_proj, ln2_g, ln2_b):
    batch, seq, d = x.shape
    depth = w_in.shape[0]
    m = batch * seq
    n_heads = d // HEAD_DIM
    alpha = (2 * depth) ** 0.25
    d_q, d_qi = n_heads * HEAD_DIM, IDX_HEADS * IDX_DIM
    assert d_q == d and w_br_rnn.shape[1] == d and (d_q + 4 * d) % d_qi == 0

    sizes = (d_q, KV_DIM, KV_DIM, d_qi, IDX_DIM, IDX_HEADS, d, d, d, d)
    offs = np.concatenate([[0], np.cumsum(sizes)])
    col = lambda w, k: w[:, offs[k]:offs[k + 1]]
    q_scale = HEAD_DIM ** -0.5 * LOG2E
    big_scale = jnp.concatenate([jnp.full((1, d_q), q_scale, F32),
                                 jnp.ones((1, 4 * d + d_qi), F32)], axis=1)
    small_scale = jnp.ones((1, 2 * LANES), F32)
    xr_block, yr_block, ga_block, gr_block = 1, 2, 3, 4
    qi_block = (d_q + 4 * d) // d_qi

    near_bias, far_q = _bias_tables(rel_bias)

    tm = _pick(m, (1024, 512, 256, 128))
    tn_big = _pick(d_q + 4 * d + d_qi, (512, 256, 128))
    ts = _pick(seq, (256, 128))
    tm_res = _pick(m, (512, 256, 128))

    xf = x.reshape(m, d)
    xb = xf.astype(BF16)
    p_all = p.reshape(depth * m, p.shape[-1])

    for i in range(depth):
        w = w_in[i]
        w_big = jnp.concatenate([col(w, 0), col(w, 6), col(w, 7), col(w, 8), col(w, 9), col(w, 3)],
                                axis=1).astype(BF16)
        w_small = jnp.concatenate(
            [col(w, 1), col(w, 2), col(w, 4), col(w, 5),
             jnp.zeros((d, 2 * LANES - 2 * KV_DIM - IDX_DIM - IDX_HEADS), w.dtype)], axis=1).astype(BF16)
        zb = _matmul(xb, w_big, big_scale, BF16, _pick(m, (2 * tm, tm)), tn_big)
        zs = _matmul(xb, w_small, small_scale, F32, tm, 2 * LANES)

        o_attn = _attention(zs, zb, near_bias, far_q, batch, seq, n_heads, qi_block)
        o_rnn = _rglru(zb, conv_w[i], conv_b[i], lru_wa[i], lru_ba[i], lru_wx[i], lru_bx[i],
                       lru_lam[i], batch, seq, d, xr_block, yr_block, ts)
        x1, x1b = _merge_ln(o_attn, o_rnn, zb, xf, w_br_attn[i].astype(BF16),
                            w_br_rnn[i].astype(BF16), w_o[i].astype(BF16), ln1_g[i], ln1_b[i],
                            alpha, ga_block, gr_block, tm_res)

        j = i // 2
        if i % 2 == 0:
            f_dim = ffn_w_gate.shape[-1]
            tf = _pick(f_dim, (512, f_dim // 2))
            n_tiles = m // tm
            f_parts = [_ffn(x1b, jnp.zeros((n_tiles,), I32), jnp.full((1,), n_tiles, I32),
                            ffn_w_gate[j][None].astype(BF16), ffn_w_up[j][None].astype(BF16),
                            ffn_w_down[j][None].astype(BF16), tm, tf)]
        else:
            f_dim = moe_w_gate.shape[-1]
            tf = _pick(f_dim, (512, f_dim // 2))
            idx, gates = _router(x1, moe_router[j], moe_router_b[j], tm_res)
            src, pos, tile_expert, n_live = _dispatch_plan(idx[:, :TOP_K], tm)
            all_experts = lambda w: w.reshape((-1,) + w.shape[2:])
            y = _ffn(x1b[src], tile_expert + j * N_EXPERTS, n_live, all_experts(moe_w_gate),
                     all_experts(moe_w_up), all_experts(moe_w_down), tm, tf)
            f_parts = [y[pos[:, 0]], y[pos[:, 1]], gates]

        xf, xb = _ple_ln(x1, x1b, p_all, i, ple_w_gate[i].astype(BF16), ple_w_proj[i].astype(BF16),
                         ln2_g[i], ln2_b[i], f_parts, alpha, tm_res)

    return xf.reshape(batch, seq, d)


@jax.jit
def kernel(x, p, w_in, conv_w, conv_b, lru_wa, lru_ba, lru_wx, lru_bx, lru_lam, w_br_attn, w_br_rnn,
           w_o, rel_bias, ln1_g, ln1_b, ffn_w_gate, ffn_w_up, ffn_w_down, moe_router, moe_router_b,
           moe_w_gate, moe_w_up, moe_w_down, ple_w_gate, ple_w_proj, ln2_g, ln2_b):
    return _forward(x, p, w_in, conv_w, conv_b, lru_wa, lru_ba, lru_wx, lru_bx, lru_lam, w_br_attn,
                    w_br_rnn, w_o, rel_bias, ln1_g, ln1_b, ffn_w_gate, ffn_w_up, ffn_w_down,
                    moe_router, moe_router_b, moe_w_gate, moe_w_up, moe_w_down, ple_w_gate,
                    ple_w_proj, ln2_g, ln2_b)
```

```python
import functools
import math

import jax
import jax.numpy as jnp
import numpy as np
from jax import lax
from jax.experimental import pallas as pl
from jax.experimental.pallas import tpu as pltpu

CHUNK = 64
CHUNK_SHIFT = CHUNK.bit_length() - 1
Q_BLOCK = 128
HEAD_DIM = 64
KV_DIM = 64
IDX_HEADS = 8
IDX_DIM = 64
INDEX_TOPK = 256
RNN_BLOCKS = 8
CONV_W = 4
LRU_C = 8.0
REL_BUCKETS = 32
REL_MAX_DIST = 128
N_EXPERTS = 8
TOP_K = 2
LN_EPS = 1e-5

LANES = 128
SUBLANES = 8
VMEM_LIMIT_BYTES = 56 * 1024 * 1024

KEY_TILE = 2 * Q_BLOCK
COUNT_TILE = 2 * KEY_TILE
HEADS_PER_DOT = 2
N_BIAS_LANES = 3
NORM_SLACK = 1.02
MAX_LOGIT_SPAN = 120.0
DOWN_SPLIT = 4
INT_MIN = -(2 ** 31)
NEG_BIG = -1e30
LOG2E = math.log2(math.e)

BF16 = jnp.bfloat16
F32 = jnp.float32
I32 = jnp.int32

_NT_DIMS = (((1,), (1,)), ((), ()))


def _params(n_axes):
    return pltpu.CompilerParams(
        dimension_semantics=("arbitrary",) * n_axes,
        vmem_limit_bytes=VMEM_LIMIT_BYTES)


def _mm_kernel(x_ref, w_ref, s_ref, o_ref):
    acc = jnp.dot(x_ref[...], w_ref[...], preferred_element_type=F32)
    o_ref[...] = (acc * s_ref[...]).astype(o_ref.dtype)


def _matmul(x, w, scale, out_dtype, tm, tn):
    m, k = x.shape
    n = w.shape[1]
    return pl.pallas_call(
        _mm_kernel,
        grid=(m // tm, n // tn),
        in_specs=[pl.BlockSpec((tm, k), lambda i, j: (i, 0)),
                  pl.BlockSpec((k, tn), lambda i, j: (0, j)),
                  pl.BlockSpec((1, tn), lambda i, j: (0, j))],
        out_specs=pl.BlockSpec((tm, tn), lambda i, j: (i, j)),
        out_shape=jax.ShapeDtypeStruct((m, n), out_dtype),
        compiler_params=_params(2),
        name="proj_in",
    )(x, w, scale)


def _fold_rows(x, op, rows):
    parts = [x[r * rows:(r + 1) * rows, :] for r in range(x.shape[0] // rows)]
    while len(parts) > 1:
        nxt = [op(parts[j], parts[j + 1]) for j in range(0, len(parts) - 1, 2)]
        if len(parts) % 2:
            nxt.append(parts[-1])
        parts = nxt
    return parts[0]


def _fold8(x, op):
    return _fold_rows(x, op, SUBLANES)


def _attn_kernel(zs_ref, q_ref, qi_ref, aux_ref, hsel_ref, nb_ref, o_ref,
                 kp_s, ki_s, vt_s, q_s, qi_s, skey_s, mb_s, l_s, acc_s, ot_s, kmax_s,
                 *, k_top, n_heads, pos_bits):
    i = pl.program_id(1)
    seq = zs_ref.shape[0]
    n_dots = n_heads // HEADS_PER_DOT
    dot_w = HEADS_PER_DOT * Q_BLOCK
    n_aug = N_BIAS_LANES * n_heads
    start = pl.multiple_of(i * Q_BLOCK, Q_BLOCK)
    n_att = (i + 2) // 2
    n_cnt = (n_att + 1) // 2

    @pl.when(i == 0)
    def _():
        kv = zs_ref[:, 0:2 * KV_DIM]
        col = lax.broadcasted_iota(I32, kv.shape, 1)
        ones = jnp.where(col < KV_DIM + n_aug, 1.0, 0.0)
        kp = jnp.where(col < KV_DIM, kv, ones).astype(BF16)
        kp_s[...] = kp
        k_sq = jnp.where(col < KV_DIM, jnp.square(kp.astype(F32)), 0.0)
        kmax_s[0] = jnp.sqrt(jnp.max(jnp.sum(k_sq, axis=1, keepdims=True)))
        vt_s[...] = kv.T[KV_DIM:2 * KV_DIM, :].astype(BF16)
        ki_s[...] = zs_ref[:, 2 * KV_DIM:2 * KV_DIM + IDX_DIM].astype(BF16)
        skey_s[...] = jnp.full(skey_s.shape, INT_MIN, I32)

    qf = q_ref[...].astype(F32)
    q_sq = jnp.dot(jnp.square(qf).astype(BF16), hsel_ref[...], preferred_element_type=F32)
    q_norm = jnp.sqrt(q_sq * NORM_SLACK)
    reach = q_norm * kmax_s[0]
    c_left = aux_ref[0:1, :] - (reach + aux_ref[1:2, :])
    safe = jnp.max(2.0 * reach + aux_ref[2:3, :]) <= MAX_LOGIT_SPAN
    terms = []
    for _ in range(N_BIAS_LANES):
        terms.append(c_left.astype(BF16).astype(F32))
        c_left = c_left - terms[-1]
    aug = jnp.concatenate([t[:, 0:n_heads] for t in terms]
                          + [jnp.zeros((Q_BLOCK, LANES - HEAD_DIM - n_aug), F32)], axis=1)
    aug_lane = lax.broadcasted_iota(I32, aug.shape, 1)

    for h in range(n_heads):
        rows = slice(h * Q_BLOCK, (h + 1) * Q_BLOCK)
        q_s[rows, 0:HEAD_DIM] = q_ref[:, h * HEAD_DIM:(h + 1) * HEAD_DIM]
        own = (aug_lane < n_aug) & ((aug_lane & (n_heads - 1)) == h)
        q_s[rows, HEAD_DIM:] = jnp.where(own, aug, 0.0).astype(BF16)
    for h in range(IDX_HEADS):
        qi_s[h * Q_BLOCK:(h + 1) * Q_BLOCK, :] = qi_ref[:, h * IDX_DIM:(h + 1) * IDX_DIM]

    w_t = zs_ref[pl.ds(start, Q_BLOCK), LANES:2 * LANES].T

    row = lax.broadcasted_iota(I32, (KEY_TILE, Q_BLOCK), 0)
    lane = lax.broadcasted_iota(I32, (KEY_TILE, Q_BLOCK), 1)
    row_cnt = lax.broadcasted_iota(I32, (COUNT_TILE, Q_BLOCK), 0)
    q_chunk = lax.shift_right_logical(start + lane, CHUNK_SHIFT)

    def score_tile(g, carry):
        off = pl.multiple_of(g * KEY_TILE, KEY_TILE)
        ki_t = ki_s[pl.ds(off, KEY_TILE), :]
        acc = jnp.zeros((KEY_TILE, Q_BLOCK), F32)
        for c in range(IDX_HEADS // HEADS_PER_DOT):
            d = lax.dot_general(ki_t, qi_s[c * dot_w:(c + 1) * dot_w, :], _NT_DIMS,
                                preferred_element_type=F32)
            for s in range(HEADS_PER_DOT):
                h = c * HEADS_PER_DOT + s
                acc = acc + (jnp.maximum(d[:, s * Q_BLOCK:(s + 1) * Q_BLOCK], 0.0)
                             * w_t[IDX_DIM + h:IDX_DIM + h + 1, :])
        bits = lax.bitcast_convert_type(acc, I32)
        key = bits ^ ((bits >> 31) & jnp.int32(0x7FFFFFFF))
        k_chunk = lax.shift_right_logical(off + row, CHUNK_SHIFT)
        skey_s[pl.ds(off, KEY_TILE), :] = jnp.where(k_chunk <= q_chunk, key, INT_MIN)
        return carry

    lax.fori_loop(0, n_att, score_tile, 0)

    def count(pred):
        def body(g, c):
            off = pl.multiple_of(g * COUNT_TILE, COUNT_TILE)
            hit = pred(skey_s[pl.ds(off, COUNT_TILE), :], off + row_cnt)
            return c + _fold8(jnp.where(hit, 1, 0).astype(I32), jnp.add)
        c = lax.fori_loop(0, n_cnt, body, jnp.zeros((SUBLANES, Q_BLOCK), I32))
        return jnp.sum(c, axis=0, keepdims=True)

    def bisect(it, t):
        cand = t ^ lax.shift_left(jnp.int32(1), 31 - it)
        c = count(lambda key, pos: key >= cand)
        return jnp.where(c >= k_top, cand, t)

    t = lax.fori_loop(0, 32, bisect, jnp.full((1, Q_BLOCK), INT_MIN, I32))

    n_ge = count(lambda key, pos: key >= t)

    @pl.when(jnp.max(n_ge) > k_top)
    def _():
        n_gt = count(lambda key, pos: key > t)
        r_m1 = k_top - n_gt - 1

        def pos_search(it, j):
            cand = j | lax.shift_left(jnp.int32(1), pos_bits - 1 - it)
            c = count(lambda key, pos: (key == t) & (pos < cand))
            return jnp.where(c <= r_m1, cand, j)

        j_cut = lax.fori_loop(0, pos_bits, pos_search, jnp.zeros((1, Q_BLOCK), I32))

        def drop(g, carry):
            off = pl.multiple_of(g * KEY_TILE, KEY_TILE)
            key = skey_s[pl.ds(off, KEY_TILE), :]
            cut = (key == t) & ((off + row) > j_cut)
            skey_s[pl.ds(off, KEY_TILE), :] = jnp.where(cut, INT_MIN, key)
            return carry

        lax.fori_loop(0, n_att, drop, 0)

    t_sel = jnp.maximum(t, INT_MIN + 1)

    def mask_tile(g, carry):
        off = pl.multiple_of(g * KEY_TILE, KEY_TILE)
        mb_s[pl.ds(off, KEY_TILE), :] = jnp.where(
            skey_s[pl.ds(off, KEY_TILE), :] >= t_sel, 0.0, NEG_BIG)
        return carry

    lax.fori_loop(0, n_att, mask_tile, 0)

    n_near = jnp.where((i & 1) == 0, jnp.minimum(n_att, 2), 1)
    n_far = n_att - n_near
    acc_s[...] = jnp.zeros(acc_s.shape, F32)
    l8_zero = tuple(jnp.zeros((SUBLANES, Q_BLOCK), F32) for _ in range(n_heads))

    def shifted_logits(lg, s, h, mb, table):
        x = lg[:, s * Q_BLOCK:(s + 1) * Q_BLOCK] + mb
        return x if table is None else x + nb_ref[table, h]

    def single_pass():
        def tile(g, l8, table):
            off = pl.multiple_of(g * KEY_TILE, KEY_TILE)
            kp_t = kp_s[pl.ds(off, KEY_TILE), :]
            vt_t = vt_s[:, pl.ds(off, KEY_TILE)]
            mb = mb_s[pl.ds(off, KEY_TILE), :]
            lgs = [lax.dot_general(kp_t, q_s[c * dot_w:(c + 1) * dot_w, :], _NT_DIMS,
                                   preferred_element_type=F32) for c in range(n_dots)]
            out = []
            for c in range(n_dots):
                ps = []
                for s in range(HEADS_PER_DOT):
                    h = c * HEADS_PER_DOT + s
                    p = jnp.exp2(shifted_logits(lgs[c], s, h, mb, table))
                    out.append(l8[h] + _fold8(p, jnp.add))
                    ps.append(p.astype(BF16))
                acc_s[c] += jnp.dot(vt_t, jnp.concatenate(ps, axis=1), preferred_element_type=F32)
            return tuple(out)

        l8 = lax.fori_loop(0, n_far, lambda g, l: tile(g, l, None), l8_zero)
        return lax.fori_loop(n_far, n_att, lambda g, l: tile(g, l, 2 * g - i + 2), l8)

    def two_pass():
        def logits_tile(g, m8, table):
            off = pl.multiple_of(g * KEY_TILE, KEY_TILE)
            kp_t = kp_s[pl.ds(off, KEY_TILE), :]
            mb = mb_s[pl.ds(off, KEY_TILE), :]
            out = []
            for c in range(n_dots):
                lg = lax.dot_general(kp_t, q_s[c * dot_w:(c + 1) * dot_w, :], _NT_DIMS,
                                     preferred_element_type=F32)
                for s in range(HEADS_PER_DOT):
                    h = c * HEADS_PER_DOT + s
                    x = shifted_logits(lg, s, h, mb, table)
                    l_s[h, pl.ds(off, KEY_TILE), :] = x
                    out.append(jnp.maximum(m8[h], _fold8(x, jnp.maximum)))
            return tuple(out)

        m8 = tuple(jnp.full((SUBLANES, Q_BLOCK), NEG_BIG, F32) for _ in range(n_heads))
        m8 = lax.fori_loop(0, n_far, lambda g, m: logits_tile(g, m, None), m8)
        m8 = lax.fori_loop(n_far, n_att, lambda g, m: logits_tile(g, m, 2 * g - i + 2), m8)
        m_row = [jnp.max(m, axis=0, keepdims=True) for m in m8]

        def pv_tile(g, l8):
            off = pl.multiple_of(g * KEY_TILE, KEY_TILE)
            vt_t = vt_s[:, pl.ds(off, KEY_TILE)]
            out = []
            for c in range(n_dots):
                ps = []
                for s in range(HEADS_PER_DOT):
                    h = c * HEADS_PER_DOT + s
                    p = jnp.exp2(l_s[h, pl.ds(off, KEY_TILE), :] - m_row[h])
                    out.append(l8[h] + _fold8(p, jnp.add))
                    ps.append(p.astype(BF16))
                acc_s[c] += jnp.dot(vt_t, jnp.concatenate(ps, axis=1), preferred_element_type=F32)
            return tuple(out)

        return lax.fori_loop(0, n_att, pv_tile, l8_zero)

    l8 = lax.cond(safe, single_pass, two_pass)

    for c in range(n_dots):
        acc = acc_s[c]
        for s in range(HEADS_PER_DOT):
            h = c * HEADS_PER_DOT + s
            denom = jnp.sum(l8[h], axis=0, keepdims=True)
            ot_s[h * HEAD_DIM:(h + 1) * HEAD_DIM, :] = acc[:, s * Q_BLOCK:(s + 1) * Q_BLOCK] / denom
    o_ref[...] = ot_s[...].T.astype(o_ref.dtype)


def _attention(zs, zb, near_bias, aux, batch, seq, n_heads, qi_block):
    d_q = n_heads * HEAD_DIM
    nqb = seq // Q_BLOCK
    k_top = min(INDEX_TOPK, seq // 4)
    assert seq % COUNT_TILE == 0 and n_heads % HEADS_PER_DOT == 0
    assert n_heads & (n_heads - 1) == 0 and N_BIAS_LANES * n_heads <= LANES - HEAD_DIM
    hsel = (np.arange(d_q)[:, None] // HEAD_DIM == np.arange(LANES)[None, :]).astype(np.float32)
    kern = functools.partial(_attn_kernel, k_top=k_top, n_heads=n_heads,
                             pos_bits=max(1, (seq - 1).bit_length()))
    return pl.pallas_call(
        kern,
        grid=(batch, nqb),
        in_specs=[
            pl.BlockSpec((seq, 2 * LANES), lambda b, i: (b, 0)),
            pl.BlockSpec((Q_BLOCK, d_q), lambda b, i: (b * nqb + i, 0)),
            pl.BlockSpec((Q_BLOCK, IDX_HEADS * IDX_DIM), lambda b, i: (b * nqb + i, qi_block)),
            pl.BlockSpec(aux.shape, lambda b, i: (0, 0)),
            pl.BlockSpec(hsel.shape, lambda b, i: (0, 0)),
            pl.BlockSpec(near_bias.shape, lambda b, i: (0, 0, 0, 0)),
        ],
        out_specs=pl.BlockSpec((Q_BLOCK, d_q), lambda b, i: (b * nqb + i, 0)),
        out_shape=jax.ShapeDtypeStruct((batch * seq, d_q), BF16),
        scratch_shapes=[
            pltpu.VMEM((seq, LANES), BF16),
            pltpu.VMEM((seq, IDX_DIM), BF16),
            pltpu.VMEM((KV_DIM, seq), BF16),
            pltpu.VMEM((n_heads * Q_BLOCK, LANES), BF16),
            pltpu.VMEM((IDX_HEADS * Q_BLOCK, IDX_DIM), BF16),
            pltpu.VMEM((seq, Q_BLOCK), I32),
            pltpu.VMEM((seq, Q_BLOCK), F32),
            pltpu.VMEM((n_heads, seq, Q_BLOCK), F32),
            pltpu.VMEM((n_heads // HEADS_PER_DOT, KV_DIM, HEADS_PER_DOT * Q_BLOCK), F32),
            pltpu.VMEM((d_q, Q_BLOCK), F32),
            pltpu.SMEM((1,), F32),
        ],
        compiler_params=_params(2),
        name="dsa_attention",
    )(zs, zb, zb, aux, jnp.asarray(hsel, BF16), near_bias)


def _t5_bucket(rel):
    half = REL_BUCKETS // 2
    max_exact = half // 2
    ret = jnp.where(rel > 0, half, 0)
    n = jnp.abs(rel)
    nf = jnp.maximum(n, 1).astype(F32)
    large = max_exact + (jnp.log(nf / max_exact) / math.log(REL_MAX_DIST / max_exact)
                         * (half - max_exact)).astype(I32)
    large = jnp.minimum(large, half - 1)
    return ret + jnp.where(n < max_exact, n, large)


def _bias_tables(rel_bias):
    n_heads = rel_bias.shape[1]
    bias2 = rel_bias.astype(F32) * LOG2E
    far = bias2[_t5_bucket(jnp.asarray(-2 * REL_MAX_DIST, I32))]
    b_max = jnp.max(bias2, axis=0)
    aux = jnp.zeros((SUBLANES, LANES), F32).at[0:3, :n_heads].set(
        jnp.stack([far, b_max, b_max - jnp.min(bias2, axis=0)]))
    delta = (np.arange(3)[:, None, None] - 2) * Q_BLOCK
    rel = delta + np.arange(KEY_TILE)[None, :, None] - np.arange(Q_BLOCK)[None, None, :]
    near = bias2[_t5_bucket(jnp.asarray(rel, I32))]
    near = jnp.moveaxis(near, -1, 1) - far[None, :, None, None]
    return near, aux


def _neg_expm1(y):
    series = -y * (1.0 + y * (1.0 / 2 + y * (1.0 / 6 + y * (1.0 / 24 + y * (1.0 / 120)))))
    return jnp.where(y > -0.1, series, 1.0 - jnp.exp(y))


def _rglru_kernel(xr_ref, yr_ref, cw_ref, cb_ref, wa_ref, ba_ref, wx_ref, bx_ref, lam_ref,
                  o_ref, ext_s, a_s, u_s, h_s, *, ts, bw):
    j = pl.program_id(1)

    @pl.when(j == 0)
    def _():
        ext_s[0:SUBLANES, :] = jnp.zeros((SUBLANES, ext_s.shape[1]), F32)
        h_s[...] = jnp.zeros(h_s.shape, F32)

    x = xr_ref[...].astype(F32)
    ext_s[SUBLANES:SUBLANES + ts, :] = x
    xc = ext_s[SUBLANES - 3:SUBLANES - 3 + ts, :] * cw_ref[0:1, :]
    xc = xc + ext_s[SUBLANES - 2:SUBLANES - 2 + ts, :] * cw_ref[1:2, :]
    xc = xc + ext_s[SUBLANES - 1:SUBLANES - 1 + ts, :] * cw_ref[2:3, :]
    xc = xc + x * cw_ref[3:4, :]
    xc = xc + cb_ref[...]
    ext_s[0:SUBLANES, :] = ext_s[ts:ts + SUBLANES, :]

    sp = jax.nn.softplus(-lam_ref[...])
    for n in range(RNN_BLOCKS):
        cs = slice(n * bw, (n + 1) * bw)
        xb = xc[:, cs]
        xb16 = xb.astype(BF16)
        r = jax.nn.sigmoid(jnp.dot(xb16, wa_ref[n], preferred_element_type=F32) + ba_ref[:, cs])
        g = jax.nn.sigmoid(jnp.dot(xb16, wx_ref[n], preferred_element_type=F32) + bx_ref[:, cs])
        log_a = -LRU_C * r * sp[:, cs]
        a_s[:, cs] = jnp.exp(log_a)
        u_s[:, cs] = jnp.sqrt(_neg_expm1(2.0 * log_a)) * (g * xb)

    def step(t, h):
        h = a_s[pl.ds(t, 1), :] * h + u_s[pl.ds(t, 1), :]
        u_s[pl.ds(t, 1), :] = h
        return h

    h_s[...] = lax.fori_loop(0, ts, step, h_s[...], unroll=8)
    o_ref[...] = (u_s[...] * jax.nn.gelu(yr_ref[...].astype(F32))).astype(o_ref.dtype)


def _rglru(zb, conv_w, conv_b, wa, ba, wx, bx, lam, batch, seq, width, xr_block, yr_block, ts):
    nts = seq // ts
    bw = width // RNN_BLOCKS
    row = lambda v: v.reshape(1, width)
    full = lambda shape: pl.BlockSpec(shape, lambda b, j: (0,) * len(shape))
    return pl.pallas_call(
        functools.partial(_rglru_kernel, ts=ts, bw=bw),
        grid=(batch, nts),
        in_specs=[pl.BlockSpec((ts, width), lambda b, j: (b * nts + j, xr_block)),
                  pl.BlockSpec((ts, width), lambda b, j: (b * nts + j, yr_block)),
                  full((CONV_W, width)), full((1, width)),
                  full((RNN_BLOCKS, bw, bw)), full((1, width)),
                  full((RNN_BLOCKS, bw, bw)), full((1, width)), full((1, width))],
        out_specs=pl.BlockSpec((ts, width), lambda b, j: (b * nts + j, 0)),
        out_shape=jax.ShapeDtypeStruct((batch * seq, width), BF16),
        scratch_shapes=[pltpu.VMEM((ts + 2 * SUBLANES, width), F32),
                        pltpu.VMEM((ts, width), F32),
                        pltpu.VMEM((ts, width), F32),
                        pltpu.VMEM((1, width), F32)],
        compiler_params=_params(2),
        name="rglru",
    )(zb, zb, conv_w, row(conv_b), wa.astype(BF16), row(ba), wx.astype(BF16), row(bx), row(lam))


def _layer_norm(y, g, b):
    mu = jnp.mean(y, axis=-1, keepdims=True)
    var = jnp.mean(jnp.square(y - mu), axis=-1, keepdims=True)
    return (y - mu) * lax.rsqrt(var + LN_EPS) * g + b


def _merge_kernel(oa_ref, or_ref, ga_ref, gr_ref, x_ref, wba_ref, wbr_ref, wo_ref, g_ref, b_ref,
                  x1_ref, x1b_ref, *, alpha):
    a = jnp.dot(oa_ref[...], wba_ref[...], preferred_element_type=F32)
    r = jnp.dot(or_ref[...], wbr_ref[...], preferred_element_type=F32)
    merged = (jax.nn.sigmoid(ga_ref[...].astype(F32)) * a
              + jax.nn.sigmoid(gr_ref[...].astype(F32)) * r)
    mix = jnp.dot(merged.astype(BF16), wo_ref[...], preferred_element_type=F32)
    x1 = _layer_norm(alpha * x_ref[...] + mix, g_ref[...], b_ref[...])
    x1_ref[...] = x1
    x1b_ref[...] = x1.astype(BF16)


def _merge_ln(oa, orn, zb, x, wba, wbr, wo, g, b, alpha, ga_block, gr_block, tm):
    m, d = x.shape
    tok = lambda blk: pl.BlockSpec((tm, d), lambda i: (i, blk))
    full = lambda shape: pl.BlockSpec(shape, lambda i: (0,) * len(shape))
    return pl.pallas_call(
        functools.partial(_merge_kernel, alpha=alpha),
        grid=(m // tm,),
        in_specs=[tok(0), tok(0), tok(ga_block), tok(gr_block), tok(0),
                  full((d, d)), full((d, d)), full((d, d)), full((1, d)), full((1, d))],
        out_specs=[tok(0), tok(0)],
        out_shape=[jax.ShapeDtypeStruct((m, d), F32), jax.ShapeDtypeStruct((m, d), BF16)],
        compiler_params=_params(1),
        name="merge_ln1",
    )(oa, orn, zb, zb, x, wba, wbr, wo, g.reshape(1, d), b.reshape(1, d))


def _ple_kernel(*refs, alpha, routed):
    if routed:
        (x1_ref, x1b_ref, p_ref, pg_ref, pp_ref, g_ref, b_ref,
         ya_ref, yb_ref, gt_ref, x2_ref, x2b_ref) = refs
        f = (gt_ref[:, 0:1] * ya_ref[...].astype(F32)
             + gt_ref[:, 1:2] * yb_ref[...].astype(F32))
    else:
        x1_ref, x1b_ref, p_ref, pg_ref, pp_ref, g_ref, b_ref, f_ref, x2_ref, x2b_ref = refs
        f = f_ref[...].astype(F32)
    gate = jax.nn.sigmoid(jnp.dot(x1b_ref[...], pg_ref[...], preferred_element_type=F32))
    proj = jnp.dot(p_ref[...].astype(BF16), pp_ref[...], preferred_element_type=F32)
    x2 = _layer_norm(alpha * x1_ref[...] + f + gate * proj, g_ref[...], b_ref[...])
    x2_ref[...] = x2
    x2b_ref[...] = x2.astype(BF16)


def _ple_ln(x1, x1b, p_all, layer, pg, pp, g, b, f_parts, alpha, tm):
    m, d = x1.shape
    pd = p_all.shape[1]
    nt = m // tm
    tok = pl.BlockSpec((tm, d), lambda i: (i, 0))
    full = lambda shape: pl.BlockSpec(shape, lambda i: (0,) * len(shape))
    routed = len(f_parts) == 3
    f_specs = [tok, tok, pl.BlockSpec((tm, LANES), lambda i: (i, 0))] if routed else [tok]
    return pl.pallas_call(
        functools.partial(_ple_kernel, alpha=alpha, routed=routed),
        grid=(nt,),
        in_specs=[tok, tok, pl.BlockSpec((tm, pd), lambda i: (layer * nt + i, 0)),
                  full((d, d)), full((pd, d)), full((1, d)), full((1, d))] + f_specs,
        out_specs=[tok, tok],
        out_shape=[jax.ShapeDtypeStruct((m, d), F32), jax.ShapeDtypeStruct((m, d), BF16)],
        compiler_params=_params(1),
        name="ple_ln2",
    )(x1, x1b, p_all, pg, pp, g.reshape(1, d), b.reshape(1, d), *f_parts)


def _ffn_kernel(te_ref, nt_ref, x_ref, wg_ref, wu_ref, wd_ref, o_ref, acc_s):
    t = pl.program_id(0)
    j = pl.program_id(1)
    last = pl.num_programs(1) - 1
    live = t < nt_ref[0]

    @pl.when(live)
    def _():
        wg = wg_ref[0].astype(BF16)
        wu = wu_ref[0].astype(BF16)
        wd = wd_ref[0].astype(BF16)
        x = x_ref[...]
        gate = jnp.dot(x, wg, preferred_element_type=F32)
        up = jnp.dot(x, wu, preferred_element_type=F32)
        hid = (jax.nn.silu(gate) * up).astype(BF16)
        cw = wd.shape[1] // DOWN_SPLIT
        for c in range(DOWN_SPLIT):
            cols = slice(c * cw, (c + 1) * cw)
            part = jnp.dot(hid, wd[:, cols], preferred_element_type=F32)
            acc_s[:, cols] = jnp.where(j == 0, part, acc_s[:, cols] + part)

    @pl.when(j == last)
    def _():
        o_ref[...] = jnp.where(live, acc_s[...], 0.0).astype(o_ref.dtype)


def _ffn(x, tile_expert, n_live, wg, wu, wd, tm, tf):
    rows, d = x.shape
    f = wg.shape[2]
    fcol = lambda t, j, te, nt: jnp.where(t < nt[0], j, 0)
    grid_spec = pltpu.PrefetchScalarGridSpec(
        num_scalar_prefetch=2,
        grid=(rows // tm, f // tf),
        in_specs=[pl.BlockSpec((tm, d), lambda t, j, te, nt: (t, 0)),
                  pl.BlockSpec((1, d, tf), lambda t, j, te, nt: (te[t], 0, fcol(t, j, te, nt))),
                  pl.BlockSpec((1, d, tf), lambda t, j, te, nt: (te[t], 0, fcol(t, j, te, nt))),
                  pl.BlockSpec((1, tf, d), lambda t, j, te, nt: (te[t], fcol(t, j, te, nt), 0))],
        out_specs=pl.BlockSpec((tm, d), lambda t, j, te, nt: (t, 0)),
        scratch_shapes=[pltpu.VMEM((tm, d), F32)])
    return pl.pallas_call(
        _ffn_kernel, grid_spec=grid_spec,
        out_shape=jax.ShapeDtypeStruct((rows, d), BF16),
        compiler_params=_params(2),
        name="swiglu_ffn",
    )(tile_expert, n_live, x, wg, wu, wd)


def _router_kernel(x_ref, w_ref, b_ref, idx_ref, gate_ref):
    lg = jnp.dot(x_ref[...], w_ref[...], preferred_element_type=F32) + b_ref[...]
    lane = lax.broadcasted_iota(I32, lg.shape, 1)
    v1 = jnp.max(lg, axis=1, keepdims=True)
    i1 = jnp.min(jnp.where(lg == v1, lane, LANES), axis=1, keepdims=True)
    lg2 = jnp.where(lane == i1, -jnp.inf, lg)
    v2 = jnp.max(lg2, axis=1, keepdims=True)
    i2 = jnp.min(jnp.where(lg2 == v2, lane, LANES), axis=1, keepdims=True)
    e2 = jnp.exp(v2 - v1)
    denom = 1.0 + e2
    idx_ref[...] = jnp.where(lane == 0, i1, jnp.where(lane == 1, i2, 0))
    gate_ref[...] = jnp.where(lane == 0, 1.0 / denom, jnp.where(lane == 1, e2 / denom, 0.0))


def _router(x1, router, router_b, tm):
    m, d = x1.shape
    w = jnp.zeros((d, LANES), F32).at[:, :N_EXPERTS].set(router.astype(F32))
    b = jnp.full((1, LANES), -jnp.inf, F32).at[0, :N_EXPERTS].set(router_b.astype(F32))
    return pl.pallas_call(
        _router_kernel,
        grid=(m // tm,),
        in_specs=[pl.BlockSpec((tm, d), lambda i: (i, 0)),
                  pl.BlockSpec((d, LANES), lambda i: (0, 0)),
                  pl.BlockSpec((1, LANES), lambda i: (0, 0))],
        out_specs=[pl.BlockSpec((tm, LANES), lambda i: (i, 0)),
                   pl.BlockSpec((tm, LANES), lambda i: (i, 0))],
        out_shape=[jax.ShapeDtypeStruct((m, LANES), I32), jax.ShapeDtypeStruct((m, LANES), F32)],
        compiler_params=_params(1),
        name="router_top2",
    )(x1, w, b)


def _dispatch_plan(top_i, tm):
    m = top_i.shape[0]
    n_assign = m * TOP_K
    e_flat = top_i.reshape(n_assign)
    onehot = (e_flat[:, None] == jnp.arange(N_EXPERTS, dtype=I32)[None, :]).astype(I32)
    csum = jnp.cumsum(onehot, axis=0)
    counts = csum[-1]
    rank = jnp.sum(csum * onehot, axis=1) - 1
    padded = ((counts + tm - 1) // tm) * tm
    ends = jnp.cumsum(padded)
    pos = (ends - padded)[e_flat] + rank
    n_rows = n_assign + N_EXPERTS * tm
    src = jnp.zeros((n_rows,), I32).at[pos].set(jnp.arange(n_assign, dtype=I32) // TOP_K)
    tile_start = jnp.arange(n_rows // tm, dtype=I32) * tm
    tile_expert = jnp.minimum(jnp.searchsorted(ends, tile_start, side="right"),
                              N_EXPERTS - 1).astype(I32)
    n_live = (ends[-1] // tm).astype(I32).reshape(1)
    return src, pos.reshape(m, TOP_K), tile_expert, n_live


def _pick(n, candidates):
    for c in candidates:
        if n % c == 0:
            return c
    return n


def _forward(x, p, w_in, conv_w, conv_b, lru_wa, lru_ba, lru_wx, lru_bx, lru_lam, w_br_attn,
             w_br_rnn, w_o, rel_bias, ln1_g, ln1_b, ffn_w_gate, ffn_w_up, ffn_w_down, moe_router,
             moe_router_b, moe_w_gate, moe_w_up, moe_w_down, ple_w_gate, ple_w_proj, ln2_g, ln2_b):
    batch, seq, d = x.shape
    depth = w_in.shape[0]
    m = batch * seq
    n_heads = d // HEAD_DIM
    alpha = (2 * depth) ** 0.25
    d_q, d_qi = n_heads * HEAD_DIM, IDX_HEADS * IDX_DIM
    assert d_q == d and w_br_rnn.shape[1] == d and (d_q + 4 * d) % d_qi == 0

    sizes = (d_q, KV_DIM, KV_DIM, d_qi, IDX_DIM, IDX_HEADS, d, d, d, d)
    offs = np.concatenate([[0], np.cumsum(sizes)])
    col = lambda w, k: w[:, offs[k]:offs[k + 1]]
    q_scale = HEAD_DIM ** -0.5 * LOG2E
    big_scale = jnp.concatenate([jnp.full((1, d_q), q_scale, F32),
                                 jnp.ones((1, 4 * d + d_qi), F32)], axis=1)
    small_scale = jnp.ones((1, 2 * LANES), F32)
    xr_block, yr_block, ga_block, gr_block = 1, 2, 3, 4
    qi_block = (d_q + 4 * d) // d_qi

    near_bias, attn_aux = _bias_tables(rel_bias)

    tm = _pick(m, (1024, 512, 256, 128))
    tn_big = _pick(d_q + 4 * d + d_qi, (512, 256, 128))
    ts = _pick(seq, (256, 128))
    tm_res = _pick(m, (512, 256, 128))

    xf = x.reshape(m, d)
    xb = xf.astype(BF16)
    p_all = p.reshape(depth * m, p.shape[-1])

    for i in range(depth):
        w = w_in[i]
        w_big = jnp.concatenate([col(w, 0), col(w, 6), col(w, 7), col(w, 8), col(w, 9), col(w, 3)],
                                axis=1).astype(BF16)
        w_small = jnp.concatenate(
            [col(w, 1), col(w, 2), col(w, 4), col(w, 5),
             jnp.zeros((d, 2 * LANES - 2 * KV_DIM - IDX_DIM - IDX_HEADS), w.dtype)], axis=1).astype(BF16)
        zb = _matmul(xb, w_big, big_scale, BF16, _pick(m, (2 * tm, tm)), tn_big)
        zs = _matmul(xb, w_small, small_scale, F32, tm, 2 * LANES)

        o_attn = _attention(zs, zb, near_bias, attn_aux, batch, seq, n_heads, qi_block)
        o_rnn = _rglru(zb, conv_w[i], conv_b[i], lru_wa[i], lru_ba[i], lru_wx[i], lru_bx[i],
                       lru_lam[i], batch, seq, d, xr_block, yr_block, ts)
        x1, x1b = _merge_ln(o_attn, o_rnn, zb, xf, w_br_attn[i].astype(BF16),
                            w_br_rnn[i].astype(BF16), w_o[i].astype(BF16), ln1_g[i], ln1_b[i],
                            alpha, ga_block, gr_block, tm_res)

        j = i // 2
        if i % 2 == 0:
            f_dim = ffn_w_gate.shape[-1]
            tf = _pick(f_dim, (512, f_dim // 2))
            n_tiles = m // tm
            f_parts = [_ffn(x1b, jnp.zeros((n_tiles,), I32), jnp.full((1,), n_tiles, I32),
                            ffn_w_gate[j][None].astype(BF16), ffn_w_up[j][None].astype(BF16),
                            ffn_w_down[j][None].astype(BF16), tm, tf)]
        else:
            f_dim = moe_w_gate.shape[-1]
            tf = _pick(f_dim, (512, f_dim // 2))
            idx, gates = _router(x1, moe_router[j], moe_router_b[j], tm_res)
            src, pos, tile_expert, n_live = _dispatch_plan(idx[:, :TOP_K], tm)
            all_experts = lambda w: w.reshape((-1,) + w.shape[2:])
            y = _ffn(x1b[src], tile_expert + j * N_EXPERTS, n_live, all_experts(moe_w_gate),
                     all_experts(moe_w_up), all_experts(moe_w_down), tm, tf)
            f_parts = [y[pos[:, 0]], y[pos[:, 1]], gates]

        xf, xb = _ple_ln(x1, x1b, p_all, i, ple_w_gate[i].astype(BF16), ple_w_proj[i].astype(BF16),
                         ln2_g[i], ln2_b[i], f_parts, alpha, tm_res)

    return xf.reshape(batch, seq, d)


@jax.jit
def kernel(x, p, w_in, conv_w, conv_b, lru_wa, lru_ba, lru_wx, lru_bx, lru_lam, w_br_attn, w_br_rnn,
           w_o, rel_bias, ln1_g, ln1_b, ffn_w_gate, ffn_w_up, ffn_w_down, moe_router, moe_router_b,
           moe_w_gate, moe_w_up, moe_w_down, ple_w_gate, ple_w_proj, ln2_g, ln2_b):
    return _forward(x, p, w_in, conv_w, conv_b, lru_wa, lru_ba, lru_wx, lru_bx, lru_lam, w_br_attn,
                    w_br_rnn, w_o, rel_bias, ln1_g, ln1_b, ffn_w_gate, ffn_w_up, ffn_w_down,
                    moe_router, moe_router_b, moe_w_gate, moe_w_up, moe_w_down, ple_w_gate,
                    ple_w_proj, ln2_g, ln2_b)
```

```python
import functools
import math

import jax
import jax.numpy as jnp
import numpy as np
from jax import lax
from jax.experimental import pallas as pl
from jax.experimental.pallas import tpu as pltpu

CHUNK = 64
CHUNK_SHIFT = CHUNK.bit_length() - 1
Q_BLOCK = 128
HEAD_DIM = 64
KV_DIM = 64
IDX_HEADS = 8
IDX_DIM = 64
INDEX_TOPK = 256
RNN_BLOCKS = 8
CONV_W = 4
LRU_C = 8.0
REL_BUCKETS = 32
REL_MAX_DIST = 128
N_EXPERTS = 8
TOP_K = 2
LN_EPS = 1e-5

LANES = 128
SUBLANES = 8
VMEM_LIMIT_BYTES = 56 * 1024 * 1024

KEY_TILE = 2 * Q_BLOCK
COUNT_TILE = 2 * KEY_TILE
HEADS_PER_DOT = 2
N_BIAS_LANES = 3
NORM_SLACK = 1.02
MAX_LOGIT_SPAN = 120.0
DOWN_SPLIT = 4
INT_MIN = -(2 ** 31)
NEG_BIG = -1e30
LOG2E = math.log2(math.e)

BF16 = jnp.bfloat16
F32 = jnp.float32
I32 = jnp.int32

_NT_DIMS = (((1,), (1,)), ((), ()))


def _params(n_axes):
    return pltpu.CompilerParams(
        dimension_semantics=("arbitrary",) * n_axes,
        vmem_limit_bytes=VMEM_LIMIT_BYTES)


def _mm_kernel(x_ref, w_ref, s_ref, o_ref):
    acc = jnp.dot(x_ref[...], w_ref[...], preferred_element_type=F32)
    o_ref[...] = (acc * s_ref[...]).astype(o_ref.dtype)


def _matmul(x, w, scale, out_dtype, tm, tn):
    m, k = x.shape
    n = w.shape[1]
    return pl.pallas_call(
        _mm_kernel,
        grid=(m // tm, n // tn),
        in_specs=[pl.BlockSpec((tm, k), lambda i, j: (i, 0)),
                  pl.BlockSpec((k, tn), lambda i, j: (0, j)),
                  pl.BlockSpec((1, tn), lambda i, j: (0, j))],
        out_specs=pl.BlockSpec((tm, tn), lambda i, j: (i, j)),
        out_shape=jax.ShapeDtypeStruct((m, n), out_dtype),
        compiler_params=_params(2),
        name="proj_in",
    )(x, w, scale)


def _fold_rows(x, op, rows):
    parts = [x[r * rows:(r + 1) * rows, :] for r in range(x.shape[0] // rows)]
    while len(parts) > 1:
        nxt = [op(parts[j], parts[j + 1]) for j in range(0, len(parts) - 1, 2)]
        if len(parts) % 2:
            nxt.append(parts[-1])
        parts = nxt
    return parts[0]


def _fold8(x, op):
    return _fold_rows(x, op, SUBLANES)


def _attn_kernel(zs_ref, q_ref, qi_ref, aux_ref, hsel_ref, nb_ref, o_ref,
                 kp_s, ki_s, vt_s, q_s, qi_s, skey_s, mb_s, l_s, acc_s, ot_s, kmax_s,
                 *, k_top, n_heads, pos_bits):
    i = pl.program_id(1)
    seq = zs_ref.shape[0]
    n_dots = n_heads // HEADS_PER_DOT
    dot_w = HEADS_PER_DOT * Q_BLOCK
    n_aug = N_BIAS_LANES * n_heads
    start = pl.multiple_of(i * Q_BLOCK, Q_BLOCK)
    n_att = (i + 2) // 2
    n_cnt = (n_att + 1) // 2

    @pl.when(i == 0)
    def _():
        kv = zs_ref[:, 0:2 * KV_DIM]
        col = lax.broadcasted_iota(I32, kv.shape, 1)
        ones = jnp.where(col < KV_DIM + n_aug, 1.0, 0.0)
        kp = jnp.where(col < KV_DIM, kv, ones).astype(BF16)
        kp_s[...] = kp
        k_sq = jnp.where(col < KV_DIM, jnp.square(kp.astype(F32)), 0.0)
        kmax_s[0] = jnp.sqrt(jnp.max(jnp.sum(k_sq, axis=1, keepdims=True)))
        vt_s[...] = kv.T[KV_DIM:2 * KV_DIM, :].astype(BF16)
        ki_s[...] = zs_ref[:, 2 * KV_DIM:2 * KV_DIM + IDX_DIM].astype(BF16)
        skey_s[...] = jnp.full(skey_s.shape, INT_MIN, I32)

    qf = q_ref[...].astype(F32)
    q_sq = jnp.dot(jnp.square(qf).astype(BF16), hsel_ref[...], preferred_element_type=F32)
    q_norm = jnp.sqrt(q_sq * NORM_SLACK)
    reach = q_norm * kmax_s[0]
    c_left = aux_ref[0:1, :] - (reach + aux_ref[1:2, :])
    safe = jnp.max(2.0 * reach + aux_ref[2:3, :]) <= MAX_LOGIT_SPAN
    terms = []
    for _ in range(N_BIAS_LANES):
        terms.append(c_left.astype(BF16).astype(F32))
        c_left = c_left - terms[-1]
    aug = jnp.concatenate([t[:, 0:n_heads] for t in terms]
                          + [jnp.zeros((Q_BLOCK, LANES - HEAD_DIM - n_aug), F32)], axis=1)
    aug_lane = lax.broadcasted_iota(I32, aug.shape, 1)

    for h in range(n_heads):
        rows = slice(h * Q_BLOCK, (h + 1) * Q_BLOCK)
        q_s[rows, 0:HEAD_DIM] = q_ref[:, h * HEAD_DIM:(h + 1) * HEAD_DIM]
        own = (aug_lane < n_aug) & ((aug_lane & (n_heads - 1)) == h)
        q_s[rows, HEAD_DIM:] = jnp.where(own, aug, 0.0).astype(BF16)
    for h in range(IDX_HEADS):
        qi_s[h * Q_BLOCK:(h + 1) * Q_BLOCK, :] = qi_ref[:, h * IDX_DIM:(h + 1) * IDX_DIM]

    w_t = zs_ref[pl.ds(start, Q_BLOCK), LANES:2 * LANES].T

    row = lax.broadcasted_iota(I32, (KEY_TILE, Q_BLOCK), 0)
    lane = lax.broadcasted_iota(I32, (KEY_TILE, Q_BLOCK), 1)
    row_cnt = lax.broadcasted_iota(I32, (COUNT_TILE, Q_BLOCK), 0)
    q_chunk = lax.shift_right_logical(start + lane, CHUNK_SHIFT)

    def score_tile(g, carry):
        off = pl.multiple_of(g * KEY_TILE, KEY_TILE)
        ki_t = ki_s[pl.ds(off, KEY_TILE), :]
        acc = jnp.zeros((KEY_TILE, Q_BLOCK), F32)
        for c in range(IDX_HEADS // HEADS_PER_DOT):
            d = lax.dot_general(ki_t, qi_s[c * dot_w:(c + 1) * dot_w, :], _NT_DIMS,
                                preferred_element_type=F32)
            for s in range(HEADS_PER_DOT):
                h = c * HEADS_PER_DOT + s
                acc = acc + (jnp.maximum(d[:, s * Q_BLOCK:(s + 1) * Q_BLOCK], 0.0)
                             * w_t[IDX_DIM + h:IDX_DIM + h + 1, :])
        bits = lax.bitcast_convert_type(acc, I32)
        key = bits ^ ((bits >> 31) & jnp.int32(0x7FFFFFFF))
        k_chunk = lax.shift_right_logical(off + row, CHUNK_SHIFT)
        skey_s[pl.ds(off, KEY_TILE), :] = jnp.where(k_chunk <= q_chunk, key, INT_MIN)
        return carry

    lax.fori_loop(0, n_att, score_tile, 0)

    def count(pred):
        def body(g, c):
            off = pl.multiple_of(g * COUNT_TILE, COUNT_TILE)
            hit = pred(skey_s[pl.ds(off, COUNT_TILE), :], off + row_cnt)
            return c + _fold8(jnp.where(hit, 1, 0).astype(I32), jnp.add)
        c = lax.fori_loop(0, n_cnt, body, jnp.zeros((SUBLANES, Q_BLOCK), I32))
        return jnp.sum(c, axis=0, keepdims=True)

    def bisect(it, t):
        cand = t ^ lax.shift_left(jnp.int32(1), 31 - it)
        c = count(lambda key, pos: key >= cand)
        return jnp.where(c >= k_top, cand, t)

    t = lax.fori_loop(0, 32, bisect, jnp.full((1, Q_BLOCK), INT_MIN, I32))

    n_ge = count(lambda key, pos: key >= t)

    @pl.when(jnp.max(n_ge) > k_top)
    def _():
        n_gt = count(lambda key, pos: key > t)
        r_m1 = k_top - n_gt - 1

        def pos_search(it, j):
            cand = j | lax.shift_left(jnp.int32(1), pos_bits - 1 - it)
            c = count(lambda key, pos: (key == t) & (pos < cand))
            return jnp.where(c <= r_m1, cand, j)

        j_cut = lax.fori_loop(0, pos_bits, pos_search, jnp.zeros((1, Q_BLOCK), I32))

        def drop(g, carry):
            off = pl.multiple_of(g * KEY_TILE, KEY_TILE)
            key = skey_s[pl.ds(off, KEY_TILE), :]
            cut = (key == t) & ((off + row) > j_cut)
            skey_s[pl.ds(off, KEY_TILE), :] = jnp.where(cut, INT_MIN, key)
            return carry

        lax.fori_loop(0, n_att, drop, 0)

    t_sel = jnp.maximum(t, INT_MIN + 1)

    def mask_tile(g, carry):
        off = pl.multiple_of(g * KEY_TILE, KEY_TILE)
        mb_s[pl.ds(off, KEY_TILE), :] = jnp.where(
            skey_s[pl.ds(off, KEY_TILE), :] >= t_sel, 0.0, NEG_BIG)
        return carry

    lax.fori_loop(0, n_att, mask_tile, 0)

    n_near = jnp.where((i & 1) == 0, jnp.minimum(n_att, 2), 1)
    n_far = n_att - n_near
    acc_s[...] = jnp.zeros(acc_s.shape, F32)
    l8_zero = tuple(jnp.zeros((SUBLANES, Q_BLOCK), F32) for _ in range(n_heads))

    def shifted_logits(lg, s, h, mb, table):
        x = lg[:, s * Q_BLOCK:(s + 1) * Q_BLOCK] + mb
        return x if table is None else x + nb_ref[table, h]

    def single_pass():
        def tile(g, l8, table):
            off = pl.multiple_of(g * KEY_TILE, KEY_TILE)
            kp_t = kp_s[pl.ds(off, KEY_TILE), :]
            vt_t = vt_s[:, pl.ds(off, KEY_TILE)]
            mb = mb_s[pl.ds(off, KEY_TILE), :]
            lgs = [lax.dot_general(kp_t, q_s[c * dot_w:(c + 1) * dot_w, :], _NT_DIMS,
                                   preferred_element_type=F32) for c in range(n_dots)]
            out = []
            for c in range(n_dots):
                ps = []
                for s in range(HEADS_PER_DOT):
                    h = c * HEADS_PER_DOT + s
                    p = jnp.exp2(shifted_logits(lgs[c], s, h, mb, table))
                    out.append(l8[h] + _fold8(p, jnp.add))
                    ps.append(p.astype(BF16))
                acc_s[c] += jnp.dot(vt_t, jnp.concatenate(ps, axis=1), preferred_element_type=F32)
            return tuple(out)

        l8 = lax.fori_loop(0, n_far, lambda g, l: tile(g, l, None), l8_zero)
        return lax.fori_loop(n_far, n_att, lambda g, l: tile(g, l, 2 * g - i + 2), l8)

    def two_pass():
        def logits_tile(g, m8, table):
            off = pl.multiple_of(g * KEY_TILE, KEY_TILE)
            kp_t = kp_s[pl.ds(off, KEY_TILE), :]
            mb = mb_s[pl.ds(off, KEY_TILE), :]
            out = []
            for c in range(n_dots):
                lg = lax.dot_general(kp_t, q_s[c * dot_w:(c + 1) * dot_w, :], _NT_DIMS,
                                     preferred_element_type=F32)
                for s in range(HEADS_PER_DOT):
                    h = c * HEADS_PER_DOT + s
                    x = shifted_logits(lg, s, h, mb, table)
                    l_s[h, pl.ds(off, KEY_TILE), :] = x
                    out.append(jnp.maximum(m8[h], _fold8(x, jnp.maximum)))
            return tuple(out)

        m8 = tuple(jnp.full((SUBLANES, Q_BLOCK), NEG_BIG, F32) for _ in range(n_heads))
        m8 = lax.fori_loop(0, n_far, lambda g, m: logits_tile(g, m, None), m8)
        m8 = lax.fori_loop(n_far, n_att, lambda g, m: logits_tile(g, m, 2 * g - i + 2), m8)
        m_row = [jnp.max(m, axis=0, keepdims=True) for m in m8]

        def pv_tile(g, l8):
            off = pl.multiple_of(g * KEY_TILE, KEY_TILE)
            vt_t = vt_s[:, pl.ds(off, KEY_TILE)]
            out = []
            for c in range(n_dots):
                ps = []
                for s in range(HEADS_PER_DOT):
                    h = c * HEADS_PER_DOT + s
                    p = jnp.exp2(l_s[h, pl.ds(off, KEY_TILE), :] - m_row[h])
                    out.append(l8[h] + _fold8(p, jnp.add))
                    ps.append(p.astype(BF16))
                acc_s[c] += jnp.dot(vt_t, jnp.concatenate(ps, axis=1), preferred_element_type=F32)
            return tuple(out)

        return lax.fori_loop(0, n_att, pv_tile, l8_zero)

    l8 = lax.cond(safe, single_pass, two_pass)

    for c in range(n_dots):
        acc = acc_s[c]
        for s in range(HEADS_PER_DOT):
            h = c * HEADS_PER_DOT + s
            denom = jnp.sum(l8[h], axis=0, keepdims=True)
            ot_s[h * HEAD_DIM:(h + 1) * HEAD_DIM, :] = acc[:, s * Q_BLOCK:(s + 1) * Q_BLOCK] / denom
    o_ref[...] = ot_s[...].T.astype(o_ref.dtype)


def _attention(zs, zb, near_bias, aux, batch, seq, n_heads, qi_block):
    d_q = n_heads * HEAD_DIM
    nqb = seq // Q_BLOCK
    k_top = min(INDEX_TOPK, seq // 4)
    assert seq % COUNT_TILE == 0 and n_heads % HEADS_PER_DOT == 0
    assert n_heads & (n_heads - 1) == 0 and N_BIAS_LANES * n_heads <= LANES - HEAD_DIM
    hsel = (np.arange(d_q)[:, None] // HEAD_DIM == np.arange(LANES)[None, :]).astype(np.float32)
    kern = functools.partial(_attn_kernel, k_top=k_top, n_heads=n_heads,
                             pos_bits=max(1, (seq - 1).bit_length()))
    return pl.pallas_call(
        kern,
        grid=(batch, nqb),
        in_specs=[
            pl.BlockSpec((seq, 2 * LANES), lambda b, i: (b, 0)),
            pl.BlockSpec((Q_BLOCK, d_q), lambda b, i: (b * nqb + i, 0)),
            pl.BlockSpec((Q_BLOCK, IDX_HEADS * IDX_DIM), lambda b, i: (b * nqb + i, qi_block)),
            pl.BlockSpec(aux.shape, lambda b, i: (0, 0)),
            pl.BlockSpec(hsel.shape, lambda b, i: (0, 0)),
            pl.BlockSpec(near_bias.shape, lambda b, i: (0, 0, 0, 0)),
        ],
        out_specs=pl.BlockSpec((Q_BLOCK, d_q), lambda b, i: (b * nqb + i, 0)),
        out_shape=jax.ShapeDtypeStruct((batch * seq, d_q), BF16),
        scratch_shapes=[
            pltpu.VMEM((seq, LANES), BF16),
            pltpu.VMEM((seq, IDX_DIM), BF16),
            pltpu.VMEM((KV_DIM, seq), BF16),
            pltpu.VMEM((n_heads * Q_BLOCK, LANES), BF16),
            pltpu.VMEM((IDX_HEADS * Q_BLOCK, IDX_DIM), BF16),
            pltpu.VMEM((seq, Q_BLOCK), I32),
            pltpu.VMEM((seq, Q_BLOCK), F32),
            pltpu.VMEM((n_heads, seq, Q_BLOCK), F32),
            pltpu.VMEM((n_heads // HEADS_PER_DOT, KV_DIM, HEADS_PER_DOT * Q_BLOCK), F32),
            pltpu.VMEM((d_q, Q_BLOCK), F32),
            pltpu.SMEM((1,), F32),
        ],
        compiler_params=_params(2),
        name="dsa_attention",
    )(zs, zb, zb, aux, jnp.asarray(hsel, BF16), near_bias)


def _t5_bucket(rel):
    half = REL_BUCKETS // 2
    max_exact = half // 2
    ret = jnp.where(rel > 0, half, 0)
    n = jnp.abs(rel)
    nf = jnp.maximum(n, 1).astype(F32)
    large = max_exact + (jnp.log(nf / max_exact) / math.log(REL_MAX_DIST / max_exact)
                         * (half - max_exact)).astype(I32)
    large = jnp.minimum(large, half - 1)
    return ret + jnp.where(n < max_exact, n, large)


def _bias_tables(rel_bias):
    n_heads = rel_bias.shape[1]
    bias2 = rel_bias.astype(F32) * LOG2E
    far = bias2[_t5_bucket(jnp.asarray(-2 * REL_MAX_DIST, I32))]
    b_max = jnp.max(bias2, axis=0)
    aux = jnp.zeros((SUBLANES, LANES), F32).at[0:3, :n_heads].set(
        jnp.stack([far, b_max, b_max - jnp.min(bias2, axis=0)]))
    delta = (np.arange(3)[:, None, None] - 2) * Q_BLOCK
    rel = delta + np.arange(KEY_TILE)[None, :, None] - np.arange(Q_BLOCK)[None, None, :]
    near = bias2[_t5_bucket(jnp.asarray(rel, I32))]
    near = jnp.moveaxis(near, -1, 1) - far[None, :, None, None]
    return near, aux


def _one_minus_sq(a, log_a):
    y = 2.0 * log_a
    series = -y * (1.0 + y * (1.0 / 2 + y * (1.0 / 6)))
    return jnp.where(y > -0.03, series, 1.0 - a * a)


def _rglru_kernel(xr_ref, yr_ref, sh_ref, cw_ref, cb_ref, wa_ref, ba_ref, wx_ref, bx_ref, lam_ref,
                  o_ref, ext_s, xc_s, a_s, u_s, h_s, *, ts, bw):
    j = pl.program_id(1)
    width = xc_s.shape[1]

    @pl.when(j == 0)
    def _():
        ext_s[0:SUBLANES, :] = jnp.zeros((SUBLANES, width), F32)
        h_s[...] = jnp.zeros(h_s.shape, F32)

    x16 = xr_ref[...]
    x = x16.astype(F32)
    taps = [jnp.dot(sh_ref[k], x16, preferred_element_type=F32) for k in range(CONV_W - 1)]
    xc = taps[0] * cw_ref[0:1, :]
    for k in range(1, CONV_W - 1):
        xc = xc + taps[k] * cw_ref[k:k + 1, :]
    xc_s[...] = xc + x * cw_ref[CONV_W - 1:CONV_W, :] + cb_ref[...]
    ext_s[SUBLANES:2 * SUBLANES, :] = x[0:SUBLANES, :]
    head = ext_s[SUBLANES - 3:2 * SUBLANES - 3, :] * cw_ref[0:1, :]
    for k in range(1, CONV_W):
        head = head + ext_s[SUBLANES - 3 + k:2 * SUBLANES - 3 + k, :] * cw_ref[k:k + 1, :]
    xc_s[0:SUBLANES, :] = head + cb_ref[...]
    ext_s[0:SUBLANES, :] = x[ts - SUBLANES:ts, :]

    sp = jax.nn.softplus(-lam_ref[...])
    for n in range(RNN_BLOCKS):
        cs = slice(n * bw, (n + 1) * bw)
        xb = xc_s[:, cs]
        xb16 = xb.astype(BF16)
        r = jax.nn.sigmoid(jnp.dot(xb16, wa_ref[n], preferred_element_type=F32) + ba_ref[:, cs])
        g = jax.nn.sigmoid(jnp.dot(xb16, wx_ref[n], preferred_element_type=F32) + bx_ref[:, cs])
        log_a = -LRU_C * r * sp[:, cs]
        a = jnp.exp(log_a)
        a_s[:, cs] = a
        u_s[:, cs] = jnp.sqrt(_one_minus_sq(a, log_a)) * (g * xb)

    row8 = lax.broadcasted_iota(I32, (SUBLANES, width), 0)

    def scan_rows(blk, h_prev):
        r0 = pl.multiple_of(blk * SUBLANES, SUBLANES)
        a_cum = a_s[pl.ds(r0, SUBLANES), :]
        u_cum = u_s[pl.ds(r0, SUBLANES), :]
        d = 1
        while d < SUBLANES:
            keep = row8 >= d
            u_cum = jnp.where(keep, a_cum * pltpu.roll(u_cum, d, 0) + u_cum, u_cum)
            a_cum = jnp.where(keep, a_cum * pltpu.roll(a_cum, d, 0), a_cum)
            d *= 2
        h = a_cum * h_prev + u_cum
        u_s[pl.ds(r0, SUBLANES), :] = h
        return h[SUBLANES - 1:SUBLANES, :]

    h_s[...] = lax.fori_loop(0, ts // SUBLANES, scan_rows, h_s[...], unroll=4)
    o_ref[...] = (u_s[...] * jax.nn.gelu(yr_ref[...].astype(F32))).astype(o_ref.dtype)


def _rglru(zb, conv_w, conv_b, wa, ba, wx, bx, lam, batch, seq, width, xr_block, yr_block, ts):
    nts = seq // ts
    bw = width // RNN_BLOCKS
    row = lambda v: v.reshape(1, width)
    full = lambda shape: pl.BlockSpec(shape, lambda b, j: (0,) * len(shape))
    t_out, t_in = np.arange(ts)[:, None], np.arange(ts)[None, :]
    shifts = np.stack([(t_out - t_in == CONV_W - 1 - k) for k in range(CONV_W - 1)]).astype(np.float32)
    return pl.pallas_call(
        functools.partial(_rglru_kernel, ts=ts, bw=bw),
        grid=(batch, nts),
        in_specs=[pl.BlockSpec((ts, width), lambda b, j: (b * nts + j, xr_block)),
                  pl.BlockSpec((ts, width), lambda b, j: (b * nts + j, yr_block)),
                  full((CONV_W - 1, ts, ts)),
                  full((CONV_W, width)), full((1, width)),
                  full((RNN_BLOCKS, bw, bw)), full((1, width)),
                  full((RNN_BLOCKS, bw, bw)), full((1, width)), full((1, width))],
        out_specs=pl.BlockSpec((ts, width), lambda b, j: (b * nts + j, 0)),
        out_shape=jax.ShapeDtypeStruct((batch * seq, width), BF16),
        scratch_shapes=[pltpu.VMEM((2 * SUBLANES, width), F32),
                        pltpu.VMEM((ts, width), F32),
                        pltpu.VMEM((ts, width), F32),
                        pltpu.VMEM((ts, width), F32),
                        pltpu.VMEM((1, width), F32)],
        compiler_params=_params(2),
        name="rglru",
    )(zb, zb, jnp.asarray(shifts, BF16), conv_w, row(conv_b), wa.astype(BF16), row(ba),
      wx.astype(BF16), row(bx), row(lam))


def _layer_norm(y, g, b):
    mu = jnp.mean(y, axis=-1, keepdims=True)
    var = jnp.mean(jnp.square(y - mu), axis=-1, keepdims=True)
    return (y - mu) * lax.rsqrt(var + LN_EPS) * g + b


def _merge_kernel(oa_ref, or_ref, ga_ref, gr_ref, x_ref, wba_ref, wbr_ref, wo_ref, g_ref, b_ref,
                  x1_ref, x1b_ref, *, alpha):
    a = jnp.dot(oa_ref[...], wba_ref[...], preferred_element_type=F32)
    r = jnp.dot(or_ref[...], wbr_ref[...], preferred_element_type=F32)
    merged = (jax.nn.sigmoid(ga_ref[...].astype(F32)) * a
              + jax.nn.sigmoid(gr_ref[...].astype(F32)) * r)
    mix = jnp.dot(merged.astype(BF16), wo_ref[...], preferred_element_type=F32)
    x1 = _layer_norm(alpha * x_ref[...] + mix, g_ref[...], b_ref[...])
    x1_ref[...] = x1
    x1b_ref[...] = x1.astype(BF16)


def _merge_ln(oa, orn, zb, x, wba, wbr, wo, g, b, alpha, ga_block, gr_block, tm):
    m, d = x.shape
    tok = lambda blk: pl.BlockSpec((tm, d), lambda i: (i, blk))
    full = lambda shape: pl.BlockSpec(shape, lambda i: (0,) * len(shape))
    return pl.pallas_call(
        functools.partial(_merge_kernel, alpha=alpha),
        grid=(m // tm,),
        in_specs=[tok(0), tok(0), tok(ga_block), tok(gr_block), tok(0),
                  full((d, d)), full((d, d)), full((d, d)), full((1, d)), full((1, d))],
        out_specs=[tok(0), tok(0)],
        out_shape=[jax.ShapeDtypeStruct((m, d), F32), jax.ShapeDtypeStruct((m, d), BF16)],
        compiler_params=_params(1),
        name="merge_ln1",
    )(oa, orn, zb, zb, x, wba, wbr, wo, g.reshape(1, d), b.reshape(1, d))


def _ple_kernel(*refs, alpha, routed):
    if routed:
        (x1_ref, x1b_ref, p_ref, pg_ref, pp_ref, g_ref, b_ref,
         ya_ref, yb_ref, gt_ref, x2_ref, x2b_ref) = refs
        f = (gt_ref[:, 0:1] * ya_ref[...].astype(F32)
             + gt_ref[:, 1:2] * yb_ref[...].astype(F32))
    else:
        x1_ref, x1b_ref, p_ref, pg_ref, pp_ref, g_ref, b_ref, f_ref, x2_ref, x2b_ref = refs
        f = f_ref[...].astype(F32)
    gate = jax.nn.sigmoid(jnp.dot(x1b_ref[...], pg_ref[...], preferred_element_type=F32))
    proj = jnp.dot(p_ref[...].astype(BF16), pp_ref[...], preferred_element_type=F32)
    x2 = _layer_norm(alpha * x1_ref[...] + f + gate * proj, g_ref[...], b_ref[...])
    x2_ref[...] = x2
    x2b_ref[...] = x2.astype(BF16)


def _ple_ln(x1, x1b, p_all, layer, pg, pp, g, b, f_parts, alpha, tm):
    m, d = x1.shape
    pd = p_all.shape[1]
    nt = m // tm
    tok = pl.BlockSpec((tm, d), lambda i: (i, 0))
    full = lambda shape: pl.BlockSpec(shape, lambda i: (0,) * len(shape))
    routed = len(f_parts) == 3
    f_specs = [tok, tok, pl.BlockSpec((tm, LANES), lambda i: (i, 0))] if routed else [tok]
    return pl.pallas_call(
        functools.partial(_ple_kernel, alpha=alpha, routed=routed),
        grid=(nt,),
        in_specs=[tok, tok, pl.BlockSpec((tm, pd), lambda i: (layer * nt + i, 0)),
                  full((d, d)), full((pd, d)), full((1, d)), full((1, d))] + f_specs,
        out_specs=[tok, tok],
        out_shape=[jax.ShapeDtypeStruct((m, d), F32), jax.ShapeDtypeStruct((m, d), BF16)],
        compiler_params=_params(1),
        name="ple_ln2",
    )(x1, x1b, p_all, pg, pp, g.reshape(1, d), b.reshape(1, d), *f_parts)


def _ffn_kernel(te_ref, nt_ref, x_ref, wg_ref, wu_ref, wd_ref, o_ref, acc_s):
    t = pl.program_id(0)
    j = pl.program_id(1)
    last = pl.num_programs(1) - 1
    live = t < nt_ref[0]

    @pl.when(live)
    def _():
        wg = wg_ref[0].astype(BF16)
        wu = wu_ref[0].astype(BF16)
        wd = wd_ref[0].astype(BF16)
        x = x_ref[...]
        gate = jnp.dot(x, wg, preferred_element_type=F32)
        up = jnp.dot(x, wu, preferred_element_type=F32)
        hid = (jax.nn.silu(gate) * up).astype(BF16)
        cw = wd.shape[1] // DOWN_SPLIT
        for c in range(DOWN_SPLIT):
            cols = slice(c * cw, (c + 1) * cw)
            part = jnp.dot(hid, wd[:, cols], preferred_element_type=F32)
            acc_s[:, cols] = jnp.where(j == 0, part, acc_s[:, cols] + part)

    @pl.when(j == last)
    def _():
        o_ref[...] = jnp.where(live, acc_s[...], 0.0).astype(o_ref.dtype)


def _ffn(x, tile_expert, n_live, wg, wu, wd, tm, tf):
    rows, d = x.shape
    f = wg.shape[2]
    fcol = lambda t, j, te, nt: jnp.where(t < nt[0], j, 0)
    grid_spec = pltpu.PrefetchScalarGridSpec(
        num_scalar_prefetch=2,
        grid=(rows // tm, f // tf),
        in_specs=[pl.BlockSpec((tm, d), lambda t, j, te, nt: (t, 0)),
                  pl.BlockSpec((1, d, tf), lambda t, j, te, nt: (te[t], 0, fcol(t, j, te, nt))),
                  pl.BlockSpec((1, d, tf), lambda t, j, te, nt: (te[t], 0, fcol(t, j, te, nt))),
                  pl.BlockSpec((1, tf, d), lambda t, j, te, nt: (te[t], fcol(t, j, te, nt), 0))],
        out_specs=pl.BlockSpec((tm, d), lambda t, j, te, nt: (t, 0)),
        scratch_shapes=[pltpu.VMEM((tm, d), F32)])
    return pl.pallas_call(
        _ffn_kernel, grid_spec=grid_spec,
        out_shape=jax.ShapeDtypeStruct((rows, d), BF16),
        compiler_params=_params(2),
        name="swiglu_ffn",
    )(tile_expert, n_live, x, wg, wu, wd)


def _router_kernel(x_ref, w_ref, b_ref, idx_ref, gate_ref):
    lg = jnp.dot(x_ref[...], w_ref[...], preferred_element_type=F32) + b_ref[...]
    lane = lax.broadcasted_iota(I32, lg.shape, 1)
    v1 = jnp.max(lg, axis=1, keepdims=True)
    i1 = jnp.min(jnp.where(lg == v1, lane, LANES), axis=1, keepdims=True)
    lg2 = jnp.where(lane == i1, -jnp.inf, lg)
    v2 = jnp.max(lg2, axis=1, keepdims=True)
    i2 = jnp.min(jnp.where(lg2 == v2, lane, LANES), axis=1, keepdims=True)
    e2 = jnp.exp(v2 - v1)
    denom = 1.0 + e2
    idx_ref[...] = jnp.where(lane == 0, i1, jnp.where(lane == 1, i2, 0))
    gate_ref[...] = jnp.where(lane == 0, 1.0 / denom, jnp.where(lane == 1, e2 / denom, 0.0))


def _router(x1, router, router_b, tm):
    m, d = x1.shape
    w = jnp.zeros((d, LANES), F32).at[:, :N_EXPERTS].set(router.astype(F32))
    b = jnp.full((1, LANES), -jnp.inf, F32).at[0, :N_EXPERTS].set(router_b.astype(F32))
    return pl.pallas_call(
        _router_kernel,
        grid=(m // tm,),
        in_specs=[pl.BlockSpec((tm, d), lambda i: (i, 0)),
                  pl.BlockSpec((d, LANES), lambda i: (0, 0)),
                  pl.BlockSpec((1, LANES), lambda i: (0, 0))],
        out_specs=[pl.BlockSpec((tm, LANES), lambda i: (i, 0)),
                   pl.BlockSpec((tm, LANES), lambda i: (i, 0))],
        out_shape=[jax.ShapeDtypeStruct((m, LANES), I32), jax.ShapeDtypeStruct((m, LANES), F32)],
        compiler_params=_params(1),
        name="router_top2",
    )(x1, w, b)


def _dispatch_plan(top_i, tm):
    m = top_i.shape[0]
    n_assign = m * TOP_K
    e_flat = top_i.reshape(n_assign)
    onehot = (e_flat[:, None] == jnp.arange(N_EXPERTS, dtype=I32)[None, :]).astype(I32)
    csum = jnp.cumsum(onehot, axis=0)
    counts = csum[-1]
    rank = jnp.sum(csum * onehot, axis=1) - 1
    padded = ((counts + tm - 1) // tm) * tm
    ends = jnp.cumsum(padded)
    pos = (ends - padded)[e_flat] + rank
    n_rows = n_assign + N_EXPERTS * tm
    src = jnp.zeros((n_rows,), I32).at[pos].set(jnp.arange(n_assign, dtype=I32) // TOP_K)
    tile_start = jnp.arange(n_rows // tm, dtype=I32) * tm
    tile_expert = jnp.minimum(jnp.searchsorted(ends, tile_start, side="right"),
                              N_EXPERTS - 1).astype(I32)
    n_live = (ends[-1] // tm).astype(I32).reshape(1)
    return src, pos.reshape(m, TOP_K), tile_expert, n_live


def _pick(n, candidates):
    for c in candidates:
        if n % c == 0:
            return c
    return n


def _forward(x, p, w_in, conv_w, conv_b, lru_wa, lru_ba, lru_wx, lru_bx, lru_lam, w_br_attn,
             w_br_rnn, w_o, rel_bias, ln1_g, ln1_b, ffn_w_gate, ffn_w_up, ffn_w_down, moe_router,
             moe_router_b, moe_w_gate, moe_w_up, moe_w_down, ple_w_gate, ple_w_proj, ln2_g, ln2_b):
    batch, seq, d = x.shape
    depth = w_in.shape[0]
    m = batch * seq
    n_heads = d // HEAD_DIM
    alpha = (2 * depth) ** 0.25
    d_q, d_qi = n_heads * HEAD_DIM, IDX_HEADS * IDX_DIM
    assert d_q == d and w_br_rnn.shape[1] == d and (d_q + 4 * d) % d_qi == 0

    sizes = (d_q, KV_DIM, KV_DIM, d_qi, IDX_DIM, IDX_HEADS, d, d, d, d)
    offs = np.concatenate([[0], np.cumsum(sizes)])
    col = lambda w, k: w[:, offs[k]:offs[k + 1]]
    q_scale = HEAD_DIM ** -0.5 * LOG2E
    big_scale = jnp.concatenate([jnp.full((1, d_q), q_scale, F32),
                                 jnp.ones((1, 4 * d + d_qi), F32)], axis=1)
    small_scale = jnp.ones((1, 2 * LANES), F32)
    xr_block, yr_block, ga_block, gr_block = 1, 2, 3, 4
    qi_block = (d_q + 4 * d) // d_qi

    near_bias, attn_aux = _bias_tables(rel_bias)

    tm = _pick(m, (1024, 512, 256, 128))
    tn_big = _pick(d_q + 4 * d + d_qi, (512, 256, 128))
    ts = _pick(seq, (256, 128))
    tm_res = _pick(m, (512, 256, 128))

    xf = x.reshape(m, d)
    xb = xf.astype(BF16)
    p_all = p.reshape(depth * m, p.shape[-1])

    for i in range(depth):
        w = w_in[i]
        w_big = jnp.concatenate([col(w, 0), col(w, 6), col(w, 7), col(w, 8), col(w, 9), col(w, 3)],
                                axis=1).astype(BF16)
        w_small = jnp.concatenate(
            [col(w, 1), col(w, 2), col(w, 4), col(w, 5),
             jnp.zeros((d, 2 * LANES - 2 * KV_DIM - IDX_DIM - IDX_HEADS), w.dtype)], axis=1).astype(BF16)
        zb = _matmul(xb, w_big, big_scale, BF16, _pick(m, (2 * tm, tm)), tn_big)
        zs = _matmul(xb, w_small, small_scale, F32, tm, 2 * LANES)

        o_attn = _attention(zs, zb, near_bias, attn_aux, batch, seq, n_heads, qi_block)
        o_rnn = _rglru(zb, conv_w[i], conv_b[i], lru_wa[i], lru_ba[i], lru_wx[i], lru_bx[i],
                       lru_lam[i], batch, seq, d, xr_block, yr_block, ts)
        x1, x1b = _merge_ln(o_attn, o_rnn, zb, xf, w_br_attn[i].astype(BF16),
                            w_br_rnn[i].astype(BF16), w_o[i].astype(BF16), ln1_g[i], ln1_b[i],
                            alpha, ga_block, gr_block, tm_res)

        j = i // 2
        if i % 2 == 0:
            f_dim = ffn_w_gate.shape[-1]
            tf = _pick(f_dim, (512, f_dim // 2))
            n_tiles = m // tm
            f_parts = [_ffn(x1b, jnp.zeros((n_tiles,), I32), jnp.full((1,), n_tiles, I32),
                            ffn_w_gate[j][None].astype(BF16), ffn_w_up[j][None].astype(BF16),
                            ffn_w_down[j][None].astype(BF16), tm, tf)]
        else:
            f_dim = moe_w_gate.shape[-1]
            tf = _pick(f_dim, (512, f_dim // 2))
            idx, gates = _router(x1, moe_router[j], moe_router_b[j], tm_res)
            src, pos, tile_expert, n_live = _dispatch_plan(idx[:, :TOP_K], tm)
            all_experts = lambda w: w.reshape((-1,) + w.shape[2:])
            y = _ffn(x1b[src], tile_expert + j * N_EXPERTS, n_live, all_experts(moe_w_gate),
                     all_experts(moe_w_up), all_experts(moe_w_down), tm, tf)
            f_parts = [y[pos[:, 0]], y[pos[:, 1]], gates]

        xf, xb = _ple_ln(x1, x1b, p_all, i, ple_w_gate[i].astype(BF16), ple_w_proj[i].astype(BF16),
                         ln2_g[i], ln2_b[i], f_parts, alpha, tm_res)

    return xf.reshape(batch, seq, d)


@jax.jit
def kernel(x, p, w_in, conv_w, conv_b, lru_wa, lru_ba, lru_wx, lru_bx, lru_lam, w_br_attn, w_br_rnn,
           w_o, rel_bias, ln1_g, ln1_b, ffn_w_gate, ffn_w_up, ffn_w_down, moe_router, moe_router_b,
           moe_w_gate, moe_w_up, moe_w_down, ple_w_gate, ple_w_proj, ln2_g, ln2_b):
    return _forward(x, p, w_in, conv_w, conv_b, lru_wa, lru_ba, lru_wx, lru_bx, lru_lam, w_br_attn,
                    w_br_rnn, w_o, rel_bias, ln1_g, ln1_b, ffn_w_gate, ffn_w_up, ffn_w_down,
                    moe_router, moe_router_b, moe_w_gate, moe_w_up, moe_w_down, ple_w_gate,
                    ple_w_proj, ln2_g, ln2_b)
```

```python
import functools
import math

import jax
import jax.numpy as jnp
import numpy as np
from jax import lax
from jax.experimental import pallas as pl
from jax.experimental.pallas import tpu as pltpu

CHUNK = 64
CHUNK_SHIFT = CHUNK.bit_length() - 1
Q_BLOCK = 128
HEAD_DIM = 64
KV_DIM = 64
IDX_HEADS = 8
IDX_DIM = 64
INDEX_TOPK = 256
RNN_BLOCKS = 8
CONV_W = 4
LRU_C = 8.0
REL_BUCKETS = 32
REL_MAX_DIST = 128
N_EXPERTS = 8
TOP_K = 2
LN_EPS = 1e-5

LANES = 128
SUBLANES = 8
VMEM_LIMIT_BYTES = 56 * 1024 * 1024

KEY_TILE = 2 * Q_BLOCK
COUNT_TILE = 2 * KEY_TILE
HEADS_PER_DOT = 2
N_BIAS_LANES = 3
NORM_SLACK = 1.02
MAX_LOGIT_SPAN = 120.0
DOWN_SPLIT = 4
INT_MIN = -(2 ** 31)
NEG_BIG = -1e30
LOG2E = math.log2(math.e)

BF16 = jnp.bfloat16
F32 = jnp.float32
I32 = jnp.int32

_NT_DIMS = (((1,), (1,)), ((), ()))


def _params(n_axes):
    return pltpu.CompilerParams(
        dimension_semantics=("arbitrary",) * n_axes,
        vmem_limit_bytes=VMEM_LIMIT_BYTES)


def _mm_kernel(x_ref, w_ref, s_ref, o_ref):
    acc = jnp.dot(x_ref[...], w_ref[...], preferred_element_type=F32)
    o_ref[...] = (acc * s_ref[...]).astype(o_ref.dtype)


def _matmul(x, w, scale, out_dtype, tm, tn):
    m, k = x.shape
    n = w.shape[1]
    return pl.pallas_call(
        _mm_kernel,
        grid=(m // tm, n // tn),
        in_specs=[pl.BlockSpec((tm, k), lambda i, j: (i, 0)),
                  pl.BlockSpec((k, tn), lambda i, j: (0, j)),
                  pl.BlockSpec((1, tn), lambda i, j: (0, j))],
        out_specs=pl.BlockSpec((tm, tn), lambda i, j: (i, j)),
        out_shape=jax.ShapeDtypeStruct((m, n), out_dtype),
        compiler_params=_params(2),
        name="proj_in",
    )(x, w, scale)


def _fold_rows(x, op, rows):
    parts = [x[r * rows:(r + 1) * rows, :] for r in range(x.shape[0] // rows)]
    while len(parts) > 1:
        nxt = [op(parts[j], parts[j + 1]) for j in range(0, len(parts) - 1, 2)]
        if len(parts) % 2:
            nxt.append(parts[-1])
        parts = nxt
    return parts[0]


def _fold8(x, op):
    return _fold_rows(x, op, SUBLANES)


def _attn_kernel(zs_ref, q_ref, qi_ref, aux_ref, hsel_ref, nb_ref, o_ref,
                 kp_s, ki_s, vt_s, q_s, qi_s, skey_s, mb_s, l_s, acc_s, ot_s, kmax_s,
                 *, k_top, n_heads, pos_bits):
    i = pl.program_id(1)
    seq = zs_ref.shape[0]
    n_dots = n_heads // HEADS_PER_DOT
    dot_w = HEADS_PER_DOT * Q_BLOCK
    n_aug = N_BIAS_LANES * n_heads
    start = pl.multiple_of(i * Q_BLOCK, Q_BLOCK)
    n_att = (i + 2) // 2
    n_cnt = (n_att + 1) // 2

    @pl.when(i == 0)
    def _():
        kv = zs_ref[:, 0:2 * KV_DIM]
        col = lax.broadcasted_iota(I32, kv.shape, 1)
        ones = jnp.where(col < KV_DIM + n_aug, 1.0, 0.0)
        kp = jnp.where(col < KV_DIM, kv, ones).astype(BF16)
        kp_s[...] = kp
        k_sq = jnp.where(col < KV_DIM, jnp.square(kp.astype(F32)), 0.0)
        kmax_s[0] = jnp.sqrt(jnp.max(jnp.sum(k_sq, axis=1, keepdims=True)))
        vt_s[...] = kv.T[KV_DIM:2 * KV_DIM, :].astype(BF16)
        ki_s[...] = zs_ref[:, 2 * KV_DIM:2 * KV_DIM + IDX_DIM].astype(BF16)
        skey_s[...] = jnp.full(skey_s.shape, INT_MIN, I32)

    qf = q_ref[...].astype(F32)
    q_sq = jnp.dot(jnp.square(qf).astype(BF16), hsel_ref[...], preferred_element_type=F32)
    q_norm = jnp.sqrt(q_sq * NORM_SLACK)
    reach = q_norm * kmax_s[0]
    c_left = aux_ref[0:1, :] - (reach + aux_ref[1:2, :])
    safe = jnp.max(2.0 * reach + aux_ref[2:3, :]) <= MAX_LOGIT_SPAN
    terms = []
    for _ in range(N_BIAS_LANES):
        terms.append(c_left.astype(BF16).astype(F32))
        c_left = c_left - terms[-1]
    aug = jnp.concatenate([t[:, 0:n_heads] for t in terms]
                          + [jnp.zeros((Q_BLOCK, LANES - HEAD_DIM - n_aug), F32)], axis=1)
    aug_lane = lax.broadcasted_iota(I32, aug.shape, 1)

    for h in range(n_heads):
        rows = slice(h * Q_BLOCK, (h + 1) * Q_BLOCK)
        q_s[rows, 0:HEAD_DIM] = q_ref[:, h * HEAD_DIM:(h + 1) * HEAD_DIM]
        own = (aug_lane < n_aug) & ((aug_lane & (n_heads - 1)) == h)
        q_s[rows, HEAD_DIM:] = jnp.where(own, aug, 0.0).astype(BF16)
    for h in range(IDX_HEADS):
        qi_s[h * Q_BLOCK:(h + 1) * Q_BLOCK, :] = qi_ref[:, h * IDX_DIM:(h + 1) * IDX_DIM]

    w_t = zs_ref[pl.ds(start, Q_BLOCK), LANES:2 * LANES].T

    row = lax.broadcasted_iota(I32, (KEY_TILE, Q_BLOCK), 0)
    lane = lax.broadcasted_iota(I32, (KEY_TILE, Q_BLOCK), 1)
    row_cnt = lax.broadcasted_iota(I32, (COUNT_TILE, Q_BLOCK), 0)
    q_chunk = lax.shift_right_logical(start + lane, CHUNK_SHIFT)

    def score_tiles(g2, carry):
        for sub in range(COUNT_TILE // KEY_TILE):
            off = pl.multiple_of(g2 * COUNT_TILE + sub * KEY_TILE, KEY_TILE)
            ki_t = ki_s[pl.ds(off, KEY_TILE), :]
            acc = jnp.zeros((KEY_TILE, Q_BLOCK), F32)
            for c in range(IDX_HEADS // HEADS_PER_DOT):
                d = lax.dot_general(ki_t, qi_s[c * dot_w:(c + 1) * dot_w, :], _NT_DIMS,
                                    preferred_element_type=F32)
                for s in range(HEADS_PER_DOT):
                    h = c * HEADS_PER_DOT + s
                    acc = acc + (jnp.maximum(d[:, s * Q_BLOCK:(s + 1) * Q_BLOCK], 0.0)
                                 * w_t[IDX_DIM + h:IDX_DIM + h + 1, :])
            bits = lax.bitcast_convert_type(acc, I32)
            key = bits ^ ((bits >> 31) & jnp.int32(0x7FFFFFFF))
            k_chunk = lax.shift_right_logical(off + row, CHUNK_SHIFT)
            skey_s[pl.ds(off, KEY_TILE), :] = jnp.where(k_chunk <= q_chunk, key, INT_MIN)
        return carry

    lax.fori_loop(0, n_cnt, score_tiles, 0)

    def count(pred):
        def body(g, c):
            off = pl.multiple_of(g * COUNT_TILE, COUNT_TILE)
            hit = pred(skey_s[pl.ds(off, COUNT_TILE), :], off + row_cnt)
            return c + _fold8(jnp.where(hit, 1, 0).astype(I32), jnp.add)
        c = lax.fori_loop(0, n_cnt, body, jnp.zeros((SUBLANES, Q_BLOCK), I32))
        return jnp.sum(c, axis=0, keepdims=True)

    def bisect(it, carry):
        t, n_ge = carry
        cand = t ^ lax.shift_left(jnp.int32(1), 31 - it)
        c = count(lambda key, pos: key >= cand)
        return jnp.where(c >= k_top, cand, t), jnp.where(c >= k_top, c, n_ge)

    t, n_ge = lax.fori_loop(
        0, 32, bisect, (jnp.full((1, Q_BLOCK), INT_MIN, I32),
                        jnp.full((1, Q_BLOCK), n_cnt * COUNT_TILE, I32)))


    @pl.when(jnp.max(n_ge) > k_top)
    def _():
        n_gt = count(lambda key, pos: key > t)
        r_m1 = k_top - n_gt - 1

        def pos_search(it, j):
            cand = j | lax.shift_left(jnp.int32(1), pos_bits - 1 - it)
            c = count(lambda key, pos: (key == t) & (pos < cand))
            return jnp.where(c <= r_m1, cand, j)

        j_cut = lax.fori_loop(0, pos_bits, pos_search, jnp.zeros((1, Q_BLOCK), I32))

        def drop(g, carry):
            off = pl.multiple_of(g * KEY_TILE, KEY_TILE)
            key = skey_s[pl.ds(off, KEY_TILE), :]
            cut = (key == t) & ((off + row) > j_cut)
            skey_s[pl.ds(off, KEY_TILE), :] = jnp.where(cut, INT_MIN, key)
            return carry

        lax.fori_loop(0, n_att, drop, 0)

    t_sel = jnp.maximum(t, INT_MIN + 1)

    def mask_tile(g, carry):
        off = pl.multiple_of(g * KEY_TILE, KEY_TILE)
        mb_s[pl.ds(off, KEY_TILE), :] = jnp.where(
            skey_s[pl.ds(off, KEY_TILE), :] >= t_sel, 0.0, NEG_BIG)
        return carry

    lax.fori_loop(0, n_att, mask_tile, 0)

    n_near = jnp.where((i & 1) == 0, jnp.minimum(n_att, 2), 1)
    n_far = n_att - n_near
    acc_s[...] = jnp.zeros(acc_s.shape, F32)
    l8_zero = tuple(jnp.zeros((SUBLANES, Q_BLOCK), F32) for _ in range(n_heads))

    def shifted_logits(lg, s, h, mb, table):
        x = lg[:, s * Q_BLOCK:(s + 1) * Q_BLOCK] + mb
        return x if table is None else x + nb_ref[table, h]

    def single_pass():
        def tile(g, l8, table):
            off = pl.multiple_of(g * KEY_TILE, KEY_TILE)
            kp_t = kp_s[pl.ds(off, KEY_TILE), :]
            vt_t = vt_s[:, pl.ds(off, KEY_TILE)]
            mb = mb_s[pl.ds(off, KEY_TILE), :]
            lgs = [lax.dot_general(kp_t, q_s[c * dot_w:(c + 1) * dot_w, :], _NT_DIMS,
                                   preferred_element_type=F32) for c in range(n_dots)]
            out = []
            for c in range(n_dots):
                ps = []
                for s in range(HEADS_PER_DOT):
                    h = c * HEADS_PER_DOT + s
                    p = jnp.exp2(shifted_logits(lgs[c], s, h, mb, table))
                    out.append(l8[h] + _fold8(p, jnp.add))
                    ps.append(p.astype(BF16))
                acc_s[c] += jnp.dot(vt_t, jnp.concatenate(ps, axis=1), preferred_element_type=F32)
            return tuple(out)

        l8 = lax.fori_loop(0, n_far, lambda g, l: tile(g, l, None), l8_zero)
        return lax.fori_loop(n_far, n_att, lambda g, l: tile(g, l, 2 * g - i + 2), l8)

    def two_pass():
        def logits_tile(g, m8, table):
            off = pl.multiple_of(g * KEY_TILE, KEY_TILE)
            kp_t = kp_s[pl.ds(off, KEY_TILE), :]
            mb = mb_s[pl.ds(off, KEY_TILE), :]
            out = []
            for c in range(n_dots):
                lg = lax.dot_general(kp_t, q_s[c * dot_w:(c + 1) * dot_w, :], _NT_DIMS,
                                     preferred_element_type=F32)
                for s in range(HEADS_PER_DOT):
                    h = c * HEADS_PER_DOT + s
                    x = shifted_logits(lg, s, h, mb, table)
                    l_s[h, pl.ds(off, KEY_TILE), :] = x
                    out.append(jnp.maximum(m8[h], _fold8(x, jnp.maximum)))
            return tuple(out)

        m8 = tuple(jnp.full((SUBLANES, Q_BLOCK), NEG_BIG, F32) for _ in range(n_heads))
        m8 = lax.fori_loop(0, n_far, lambda g, m: logits_tile(g, m, None), m8)
        m8 = lax.fori_loop(n_far, n_att, lambda g, m: logits_tile(g, m, 2 * g - i + 2), m8)
        m_row = [jnp.max(m, axis=0, keepdims=True) for m in m8]

        def pv_tile(g, l8):
            off = pl.multiple_of(g * KEY_TILE, KEY_TILE)
            vt_t = vt_s[:, pl.ds(off, KEY_TILE)]
            out = []
            for c in range(n_dots):
                ps = []
                for s in range(HEADS_PER_DOT):
                    h = c * HEADS_PER_DOT + s
                    p = jnp.exp2(l_s[h, pl.ds(off, KEY_TILE), :] - m_row[h])
                    out.append(l8[h] + _fold8(p, jnp.add))
                    ps.append(p.astype(BF16))
                acc_s[c] += jnp.dot(vt_t, jnp.concatenate(ps, axis=1), preferred_element_type=F32)
            return tuple(out)

        return lax.fori_loop(0, n_att, pv_tile, l8_zero)

    l8 = lax.cond(safe, single_pass, two_pass)

    for c in range(n_dots):
        acc = acc_s[c]
        for s in range(HEADS_PER_DOT):
            h = c * HEADS_PER_DOT + s
            denom = jnp.sum(l8[h], axis=0, keepdims=True)
            ot_s[h * HEAD_DIM:(h + 1) * HEAD_DIM, :] = acc[:, s * Q_BLOCK:(s + 1) * Q_BLOCK] / denom
    o_ref[...] = ot_s[...].T.astype(o_ref.dtype)


def _attention(zs, zb, near_bias, aux, batch, seq, n_heads, qi_block):
    d_q = n_heads * HEAD_DIM
    nqb = seq // Q_BLOCK
    k_top = min(INDEX_TOPK, seq // 4)
    assert seq % COUNT_TILE == 0 and n_heads % HEADS_PER_DOT == 0
    assert n_heads & (n_heads - 1) == 0 and N_BIAS_LANES * n_heads <= LANES - HEAD_DIM
    hsel = (np.arange(d_q)[:, None] // HEAD_DIM == np.arange(LANES)[None, :]).astype(np.float32)
    kern = functools.partial(_attn_kernel, k_top=k_top, n_heads=n_heads,
                             pos_bits=max(1, (seq - 1).bit_length()))
    return pl.pallas_call(
        kern,
        grid=(batch, nqb),
        in_specs=[
            pl.BlockSpec((seq, 2 * LANES), lambda b, i: (b, 0)),
            pl.BlockSpec((Q_BLOCK, d_q), lambda b, i: (b * nqb + i, 0)),
            pl.BlockSpec((Q_BLOCK, IDX_HEADS * IDX_DIM), lambda b, i: (b * nqb + i, qi_block)),
            pl.BlockSpec(aux.shape, lambda b, i: (0, 0)),
            pl.BlockSpec(hsel.shape, lambda b, i: (0, 0)),
            pl.BlockSpec(near_bias.shape, lambda b, i: (0, 0, 0, 0)),
        ],
        out_specs=pl.BlockSpec((Q_BLOCK, d_q), lambda b, i: (b * nqb + i, 0)),
        out_shape=jax.ShapeDtypeStruct((batch * seq, d_q), BF16),
        scratch_shapes=[
            pltpu.VMEM((seq, LANES), BF16),
            pltpu.VMEM((seq, IDX_DIM), BF16),
            pltpu.VMEM((KV_DIM, seq), BF16),
            pltpu.VMEM((n_heads * Q_BLOCK, LANES), BF16),
            pltpu.VMEM((IDX_HEADS * Q_BLOCK, IDX_DIM), BF16),
            pltpu.VMEM((seq, Q_BLOCK), I32),
            pltpu.VMEM((seq, Q_BLOCK), F32),
            pltpu.VMEM((n_heads, seq, Q_BLOCK), F32),
            pltpu.VMEM((n_heads // HEADS_PER_DOT, KV_DIM, HEADS_PER_DOT * Q_BLOCK), F32),
            pltpu.VMEM((d_q, Q_BLOCK), F32),
            pltpu.SMEM((1,), F32),
        ],
        compiler_params=_params(2),
        name="dsa_attention",
    )(zs, zb, zb, aux, jnp.asarray(hsel, BF16), near_bias)


def _t5_bucket(rel):
    half = REL_BUCKETS // 2
    max_exact = half // 2
    ret = jnp.where(rel > 0, half, 0)
    n = jnp.abs(rel)
    nf = jnp.maximum(n, 1).astype(F32)
    large = max_exact + (jnp.log(nf / max_exact) / math.log(REL_MAX_DIST / max_exact)
                         * (half - max_exact)).astype(I32)
    large = jnp.minimum(large, half - 1)
    return ret + jnp.where(n < max_exact, n, large)


def _bias_tables(rel_bias):
    n_heads = rel_bias.shape[1]
    bias2 = rel_bias.astype(F32) * LOG2E
    far = bias2[_t5_bucket(jnp.asarray(-2 * REL_MAX_DIST, I32))]
    b_max = jnp.max(bias2, axis=0)
    aux = jnp.zeros((SUBLANES, LANES), F32).at[0:3, :n_heads].set(
        jnp.stack([far, b_max, b_max - jnp.min(bias2, axis=0)]))
    rel_min = -2 * Q_BLOCK - (Q_BLOCK - 1)
    by_rel = bias2[_t5_bucket(jnp.arange(rel_min, KEY_TILE, dtype=I32))] - far[None, :]
    span = KEY_TILE + Q_BLOCK - 1
    tiles = []
    for d in range(3):
        w = lax.slice_in_dim(by_rel, d * Q_BLOCK, d * Q_BLOCK + span, axis=0)
        wrapped = jnp.tile(jnp.concatenate([w, jnp.zeros((1, n_heads), F32)]), (Q_BLOCK, 1))
        rows = wrapped[:Q_BLOCK * span].reshape(Q_BLOCK, span, n_heads)
        tiles.append(rows[:, Q_BLOCK - 1:Q_BLOCK - 1 + KEY_TILE, :])
    near = jnp.transpose(jnp.stack(tiles), (0, 3, 2, 1))
    return near, aux


def _one_minus_sq(a, log_a):
    y = 2.0 * log_a
    series = -y * (1.0 + y * (1.0 / 2 + y * (1.0 / 6)))
    return jnp.where(y > -0.03, series, 1.0 - a * a)


def _rglru_kernel(xr_ref, yr_ref, sh_ref, cw_ref, cb_ref, wa_ref, ba_ref, wx_ref, bx_ref, lam_ref,
                  o_ref, ext_s, xc_s, a_s, u_s, h_s, *, ts, bw):
    j = pl.program_id(1)
    width = xc_s.shape[1]

    @pl.when(j == 0)
    def _():
        ext_s[0:SUBLANES, :] = jnp.zeros((SUBLANES, width), F32)
        h_s[...] = jnp.zeros(h_s.shape, F32)

    x16 = xr_ref[...]
    x = x16.astype(F32)
    taps = [jnp.dot(sh_ref[k], x16, preferred_element_type=F32) for k in range(CONV_W - 1)]
    xc = taps[0] * cw_ref[0:1, :]
    for k in range(1, CONV_W - 1):
        xc = xc + taps[k] * cw_ref[k:k + 1, :]
    xc_s[...] = xc + x * cw_ref[CONV_W - 1:CONV_W, :] + cb_ref[...]
    ext_s[SUBLANES:2 * SUBLANES, :] = x[0:SUBLANES, :]
    head = ext_s[SUBLANES - 3:2 * SUBLANES - 3, :] * cw_ref[0:1, :]
    for k in range(1, CONV_W):
        head = head + ext_s[SUBLANES - 3 + k:2 * SUBLANES - 3 + k, :] * cw_ref[k:k + 1, :]
    xc_s[0:SUBLANES, :] = head + cb_ref[...]
    ext_s[0:SUBLANES, :] = x[ts - SUBLANES:ts, :]

    sp = jax.nn.softplus(-lam_ref[...])
    for n in range(RNN_BLOCKS):
        cs = slice(n * bw, (n + 1) * bw)
        xb = xc_s[:, cs]
        xb16 = xb.astype(BF16)
        r = jax.nn.sigmoid(jnp.dot(xb16, wa_ref[n], preferred_element_type=F32) + ba_ref[:, cs])
        g = jax.nn.sigmoid(jnp.dot(xb16, wx_ref[n], preferred_element_type=F32) + bx_ref[:, cs])
        log_a = -LRU_C * r * sp[:, cs]
        a = jnp.exp(log_a)
        a_s[:, cs] = a
        u_s[:, cs] = jnp.sqrt(_one_minus_sq(a, log_a)) * (g * xb)

    row8 = lax.broadcasted_iota(I32, (SUBLANES, width), 0)

    def scan_rows(blk, h_prev):
        r0 = pl.multiple_of(blk * SUBLANES, SUBLANES)
        a_cum = a_s[pl.ds(r0, SUBLANES), :]
        u_cum = u_s[pl.ds(r0, SUBLANES), :]
        d = 1
        while d < SUBLANES:
            keep = row8 >= d
            u_cum = jnp.where(keep, a_cum * pltpu.roll(u_cum, d, 0) + u_cum, u_cum)
            a_cum = jnp.where(keep, a_cum * pltpu.roll(a_cum, d, 0), a_cum)
            d *= 2
        h = a_cum * h_prev + u_cum
        u_s[pl.ds(r0, SUBLANES), :] = h
        return h[SUBLANES - 1:SUBLANES, :]

    h_s[...] = lax.fori_loop(0, ts // SUBLANES, scan_rows, h_s[...], unroll=4)
    o_ref[...] = (u_s[...] * jax.nn.gelu(yr_ref[...].astype(F32))).astype(o_ref.dtype)


def _rglru(zb, conv_w, conv_b, wa, ba, wx, bx, lam, batch, seq, width, xr_block, yr_block, ts):
    nts = seq // ts
    bw = width // RNN_BLOCKS
    row = lambda v: v.reshape(1, width)
    full = lambda shape: pl.BlockSpec(shape, lambda b, j: (0,) * len(shape))
    t_out, t_in = np.arange(ts)[:, None], np.arange(ts)[None, :]
    shifts = np.stack([(t_out - t_in == CONV_W - 1 - k) for k in range(CONV_W - 1)]).astype(np.float32)
    return pl.pallas_call(
        functools.partial(_rglru_kernel, ts=ts, bw=bw),
        grid=(batch, nts),
        in_specs=[pl.BlockSpec((ts, width), lambda b, j: (b * nts + j, xr_block)),
                  pl.BlockSpec((ts, width), lambda b, j: (b * nts + j, yr_block)),
                  full((CONV_W - 1, ts, ts)),
                  full((CONV_W, width)), full((1, width)),
                  full((RNN_BLOCKS, bw, bw)), full((1, width)),
                  full((RNN_BLOCKS, bw, bw)), full((1, width)), full((1, width))],
        out_specs=pl.BlockSpec((ts, width), lambda b, j: (b * nts + j, 0)),
        out_shape=jax.ShapeDtypeStruct((batch * seq, width), BF16),
        scratch_shapes=[pltpu.VMEM((2 * SUBLANES, width), F32),
                        pltpu.VMEM((ts, width), F32),
                        pltpu.VMEM((ts, width), F32),
                        pltpu.VMEM((ts, width), F32),
                        pltpu.VMEM((1, width), F32)],
        compiler_params=_params(2),
        name="rglru",
    )(zb, zb, jnp.asarray(shifts, BF16), conv_w, row(conv_b), wa.astype(BF16), row(ba),
      wx.astype(BF16), row(bx), row(lam))


def _layer_norm(y, g, b):
    mu = jnp.mean(y, axis=-1, keepdims=True)
    var = jnp.mean(jnp.square(y - mu), axis=-1, keepdims=True)
    return (y - mu) * lax.rsqrt(var + LN_EPS) * g + b


def _merge_kernel(oa_ref, or_ref, ga_ref, gr_ref, x_ref, wba_ref, wbr_ref, wo_ref, g_ref, b_ref,
                  x1_ref, x1b_ref, *, alpha):
    a = jnp.dot(oa_ref[...], wba_ref[...], preferred_element_type=F32)
    r = jnp.dot(or_ref[...], wbr_ref[...], preferred_element_type=F32)
    merged = (jax.nn.sigmoid(ga_ref[...].astype(F32)) * a
              + jax.nn.sigmoid(gr_ref[...].astype(F32)) * r)
    mix = jnp.dot(merged.astype(BF16), wo_ref[...], preferred_element_type=F32)
    x1 = _layer_norm(alpha * x_ref[...] + mix, g_ref[...], b_ref[...])
    x1_ref[...] = x1
    x1b_ref[...] = x1.astype(BF16)


def _merge_ln(oa, orn, zb, x, wba, wbr, wo, g, b, alpha, ga_block, gr_block, tm):
    m, d = x.shape
    tok = lambda blk: pl.BlockSpec((tm, d), lambda i: (i, blk))
    full = lambda shape: pl.BlockSpec(shape, lambda i: (0,) * len(shape))
    return pl.pallas_call(
        functools.partial(_merge_kernel, alpha=alpha),
        grid=(m // tm,),
        in_specs=[tok(0), tok(0), tok(ga_block), tok(gr_block), tok(0),
                  full((d, d)), full((d, d)), full((d, d)), full((1, d)), full((1, d))],
        out_specs=[tok(0), tok(0)],
        out_shape=[jax.ShapeDtypeStruct((m, d), F32), jax.ShapeDtypeStruct((m, d), BF16)],
        compiler_params=_params(1),
        name="merge_ln1",
    )(oa, orn, zb, zb, x, wba, wbr, wo, g.reshape(1, d), b.reshape(1, d))


def _ple_kernel(*refs, alpha, routed):
    if routed:
        (x1_ref, x1b_ref, p_ref, pg_ref, pp_ref, g_ref, b_ref,
         ya_ref, yb_ref, gt_ref, x2_ref, x2b_ref) = refs
        f = (gt_ref[:, 0:1] * ya_ref[...].astype(F32)
             + gt_ref[:, 1:2] * yb_ref[...].astype(F32))
    else:
        x1_ref, x1b_ref, p_ref, pg_ref, pp_ref, g_ref, b_ref, f_ref, x2_ref, x2b_ref = refs
        f = f_ref[...].astype(F32)
    gate = jax.nn.sigmoid(jnp.dot(x1b_ref[...], pg_ref[...], preferred_element_type=F32))
    proj = jnp.dot(p_ref[...].astype(BF16), pp_ref[...], preferred_element_type=F32)
    x2 = _layer_norm(alpha * x1_ref[...] + f + gate * proj, g_ref[...], b_ref[...])
    x2_ref[...] = x2
    x2b_ref[...] = x2.astype(BF16)


def _ple_ln(x1, x1b, p_all, layer, pg, pp, g, b, f_parts, alpha, tm):
    m, d = x1.shape
    pd = p_all.shape[1]
    nt = m // tm
    tok = pl.BlockSpec((tm, d), lambda i: (i, 0))
    full = lambda shape: pl.BlockSpec(shape, lambda i: (0,) * len(shape))
    routed = len(f_parts) == 3
    f_specs = [tok, tok, pl.BlockSpec((tm, LANES), lambda i: (i, 0))] if routed else [tok]
    return pl.pallas_call(
        functools.partial(_ple_kernel, alpha=alpha, routed=routed),
        grid=(nt,),
        in_specs=[tok, tok, pl.BlockSpec((tm, pd), lambda i: (layer * nt + i, 0)),
                  full((d, d)), full((pd, d)), full((1, d)), full((1, d))] + f_specs,
        out_specs=[tok, tok],
        out_shape=[jax.ShapeDtypeStruct((m, d), F32), jax.ShapeDtypeStruct((m, d), BF16)],
        compiler_params=_params(1),
        name="ple_ln2",
    )(x1, x1b, p_all, pg, pp, g.reshape(1, d), b.reshape(1, d), *f_parts)


def _ffn_kernel(te_ref, nt_ref, x_ref, wg_ref, wu_ref, wd_ref, o_ref, acc_s):
    t = pl.program_id(0)
    j = pl.program_id(1)
    last = pl.num_programs(1) - 1
    live = t < nt_ref[0]

    @pl.when(live)
    def _():
        wg = wg_ref[0].astype(BF16)
        wu = wu_ref[0].astype(BF16)
        wd = wd_ref[0].astype(BF16)
        x = x_ref[...]
        gate = jnp.dot(x, wg, preferred_element_type=F32)
        up = jnp.dot(x, wu, preferred_element_type=F32)
        hid = (jax.nn.silu(gate) * up).astype(BF16)
        cw = wd.shape[1] // DOWN_SPLIT
        for c in range(DOWN_SPLIT):
            cols = slice(c * cw, (c + 1) * cw)
            part = jnp.dot(hid, wd[:, cols], preferred_element_type=F32)
            acc_s[:, cols] = jnp.where(j == 0, part, acc_s[:, cols] + part)

    @pl.when(j == last)
    def _():
        o_ref[...] = jnp.where(live, acc_s[...], 0.0).astype(o_ref.dtype)


def _ffn(x, tile_expert, n_live, wg, wu, wd, tm, tf):
    rows, d = x.shape
    f = wg.shape[2]
    fcol = lambda t, j, te, nt: jnp.where(t < nt[0], j, 0)
    grid_spec = pltpu.PrefetchScalarGridSpec(
        num_scalar_prefetch=2,
        grid=(rows // tm, f // tf),
        in_specs=[pl.BlockSpec((tm, d), lambda t, j, te, nt: (t, 0)),
                  pl.BlockSpec((1, d, tf), lambda t, j, te, nt: (te[t], 0, fcol(t, j, te, nt))),
                  pl.BlockSpec((1, d, tf), lambda t, j, te, nt: (te[t], 0, fcol(t, j, te, nt))),
                  pl.BlockSpec((1, tf, d), lambda t, j, te, nt: (te[t], fcol(t, j, te, nt), 0))],
        out_specs=pl.BlockSpec((tm, d), lambda t, j, te, nt: (t, 0)),
        scratch_shapes=[pltpu.VMEM((tm, d), F32)])
    return pl.pallas_call(
        _ffn_kernel, grid_spec=grid_spec,
        out_shape=jax.ShapeDtypeStruct((rows, d), BF16),
        compiler_params=_params(2),
        name="swiglu_ffn",
    )(tile_expert, n_live, x, wg, wu, wd)


def _router_kernel(x_ref, w_ref, b_ref, idx_ref, gate_ref):
    lg = jnp.dot(x_ref[...], w_ref[...], preferred_element_type=F32) + b_ref[...]
    lane = lax.broadcasted_iota(I32, lg.shape, 1)
    v1 = jnp.max(lg, axis=1, keepdims=True)
    i1 = jnp.min(jnp.where(lg == v1, lane, LANES), axis=1, keepdims=True)
    lg2 = jnp.where(lane == i1, -jnp.inf, lg)
    v2 = jnp.max(lg2, axis=1, keepdims=True)
    i2 = jnp.min(jnp.where(lg2 == v2, lane, LANES), axis=1, keepdims=True)
    e2 = jnp.exp(v2 - v1)
    denom = 1.0 + e2
    idx_ref[...] = jnp.where(lane == 0, i1, jnp.where(lane == 1, i2, 0))
    gate_ref[...] = jnp.where(lane == 0, 1.0 / denom, jnp.where(lane == 1, e2 / denom, 0.0))


def _router(x1, router, router_b, tm):
    m, d = x1.shape
    w = jnp.zeros((d, LANES), F32).at[:, :N_EXPERTS].set(router.astype(F32))
    b = jnp.full((1, LANES), -jnp.inf, F32).at[0, :N_EXPERTS].set(router_b.astype(F32))
    return pl.pallas_call(
        _router_kernel,
        grid=(m // tm,),
        in_specs=[pl.BlockSpec((tm, d), lambda i: (i, 0)),
                  pl.BlockSpec((d, LANES), lambda i: (0, 0)),
                  pl.BlockSpec((1, LANES), lambda i: (0, 0))],
        out_specs=[pl.BlockSpec((tm, LANES), lambda i: (i, 0)),
                   pl.BlockSpec((tm, LANES), lambda i: (i, 0))],
        out_shape=[jax.ShapeDtypeStruct((m, LANES), I32), jax.ShapeDtypeStruct((m, LANES), F32)],
        compiler_params=_params(1),
        name="router_top2",
    )(x1, w, b)


def _dispatch_plan(top_i, tm):
    m = top_i.shape[0]
    n_assign = m * TOP_K
    e_flat = top_i.reshape(n_assign)
    onehot = (e_flat[:, None] == jnp.arange(N_EXPERTS, dtype=I32)[None, :]).astype(I32)
    csum = jnp.cumsum(onehot, axis=0)
    counts = csum[-1]
    rank = jnp.sum(csum * onehot, axis=1) - 1
    padded = ((counts + tm - 1) // tm) * tm
    ends = jnp.cumsum(padded)
    pos = (ends - padded)[e_flat] + rank
    n_rows = n_assign + N_EXPERTS * tm
    src = jnp.zeros((n_rows,), I32).at[pos].set(jnp.arange(n_assign, dtype=I32) // TOP_K)
    tile_start = jnp.arange(n_rows // tm, dtype=I32) * tm
    tile_expert = jnp.minimum(jnp.searchsorted(ends, tile_start, side="right"),
                              N_EXPERTS - 1).astype(I32)
    n_live = (ends[-1] // tm).astype(I32).reshape(1)
    return src, pos.reshape(m, TOP_K), tile_expert, n_live


def _pick(n, candidates):
    for c in candidates:
        if n % c == 0:
            return c
    return n


def _forward(x, p, w_in, conv_w, conv_b, lru_wa, lru_ba, lru_wx, lru_bx, lru_lam, w_br_attn,
             w_br_rnn, w_o, rel_bias, ln1_g, ln1_b, ffn_w_gate, ffn_w_up, ffn_w_down, moe_router,
             moe_router_b, moe_w_gate, moe_w_up, moe_w_down, ple_w_gate, ple_w_proj, ln2_g, ln2_b):
    batch, seq, d = x.shape
    depth = w_in.shape[0]
    m = batch * seq
    n_heads = d // HEAD_DIM
    alpha = (2 * depth) ** 0.25
    d_q, d_qi = n_heads * HEAD_DIM, IDX_HEADS * IDX_DIM
    assert d_q == d and w_br_rnn.shape[1] == d and (d_q + 4 * d) % d_qi == 0

    sizes = (d_q, KV_DIM, KV_DIM, d_qi, IDX_DIM, IDX_HEADS, d, d, d, d)
    offs = np.concatenate([[0], np.cumsum(sizes)])
    col = lambda w, k: w[:, offs[k]:offs[k + 1]]
    q_scale = HEAD_DIM ** -0.5 * LOG2E
    big_scale = jnp.concatenate([jnp.full((1, d_q), q_scale, F32),
                                 jnp.ones((1, 4 * d + d_qi), F32)], axis=1)
    small_scale = jnp.ones((1, 2 * LANES), F32)
    xr_block, yr_block, ga_block, gr_block = 1, 2, 3, 4
    qi_block = (d_q + 4 * d) // d_qi

    near_bias, attn_aux = _bias_tables(rel_bias)

    tm = _pick(m, (1024, 512, 256, 128))
    tn_big = _pick(d_q + 4 * d + d_qi, (512, 256, 128))
    ts = _pick(seq, (256, 128))
    tm_res = _pick(m, (512, 256, 128))

    xf = x.reshape(m, d)
    xb = xf.astype(BF16)
    p_all = p.reshape(depth * m, p.shape[-1])

    for i in range(depth):
        w = w_in[i]
        w_big = jnp.concatenate([col(w, 0), col(w, 6), col(w, 7), col(w, 8), col(w, 9), col(w, 3)],
                                axis=1).astype(BF16)
        w_small = jnp.concatenate(
            [col(w, 1), col(w, 2), col(w, 4), col(w, 5),
             jnp.zeros((d, 2 * LANES - 2 * KV_DIM - IDX_DIM - IDX_HEADS), w.dtype)], axis=1).astype(BF16)
        zb = _matmul(xb, w_big, big_scale, BF16, _pick(m, (2 * tm, tm)), tn_big)
        zs = _matmul(xb, w_small, small_scale, F32, tm, 2 * LANES)

        o_attn = _attention(zs, zb, near_bias, attn_aux, batch, seq, n_heads, qi_block)
        o_rnn = _rglru(zb, conv_w[i], conv_b[i], lru_wa[i], lru_ba[i], lru_wx[i], lru_bx[i],
                       lru_lam[i], batch, seq, d, xr_block, yr_block, ts)
        x1, x1b = _merge_ln(o_attn, o_rnn, zb, xf, w_br_attn[i].astype(BF16),
                            w_br_rnn[i].astype(BF16), w_o[i].astype(BF16), ln1_g[i], ln1_b[i],
                            alpha, ga_block, gr_block, tm_res)

        j = i // 2
        if i % 2 == 0:
            f_dim = ffn_w_gate.shape[-1]
            tf = _pick(f_dim, (512, f_dim // 2))
            n_tiles = m // tm
            f_parts = [_ffn(x1b, jnp.zeros((n_tiles,), I32), jnp.full((1,), n_tiles, I32),
                            ffn_w_gate[j][None].astype(BF16), ffn_w_up[j][None].astype(BF16),
                            ffn_w_down[j][None].astype(BF16), tm, tf)]
        else:
            f_dim = moe_w_gate.shape[-1]
            tf = _pick(f_dim, (512, f_dim // 2))
            idx, gates = _router(x1, moe_router[j], moe_router_b[j], tm_res)
            src, pos, tile_expert, n_live = _dispatch_plan(idx[:, :TOP_K], tm)
            all_experts = lambda w: w.reshape((-1,) + w.shape[2:])
            y = _ffn(x1b[src], tile_expert + j * N_EXPERTS, n_live, all_experts(moe_w_gate),
                     all_experts(moe_w_up), all_experts(moe_w_down), tm, tf)
            f_parts = [y[pos[:, 0]], y[pos[:, 1]], gates]

        xf, xb = _ple_ln(x1, x1b, p_all, i, ple_w_gate[i].astype(BF16), ple_w_proj[i].astype(BF16),
                         ln2_g[i], ln2_b[i], f_parts, alpha, tm_res)

    return xf.reshape(batch, seq, d)


@jax.jit
def kernel(x, p, w_in, conv_w, conv_b, lru_wa, lru_ba, lru_wx, lru_bx, lru_lam, w_br_attn, w_br_rnn,
           w_o, rel_bias, ln1_g, ln1_b, ffn_w_gate, ffn_w_up, ffn_w_down, moe_router, moe_router_b,
           moe_w_gate, moe_w_up, moe_w_down, ple_w_gate, ple_w_proj, ln2_g, ln2_b):
    return _forward(x, p, w_in, conv_w, conv_b, lru_wa, lru_ba, lru_wx, lru_bx, lru_lam, w_br_attn,
                    w_br_rnn, w_o, rel_bias, ln1_g, ln1_b, ffn_w_gate, ffn_w_up, ffn_w_down,
                    moe_router, moe_router_b, moe_w_gate, moe_w_up, moe_w_down, ple_w_gate,
                    ple_w_proj, ln2_g, ln2_b)
```

```python
import functools
import math

import jax
import jax.numpy as jnp
import numpy as np
from jax import lax
from jax.experimental import pallas as pl
from jax.experimental.pallas import tpu as pltpu

CHUNK = 64
CHUNK_SHIFT = CHUNK.bit_length() - 1
Q_BLOCK = 128
HEAD_DIM = 64
KV_DIM = 64
IDX_HEADS = 8
IDX_DIM = 64
INDEX_TOPK = 256
RNN_BLOCKS = 8
CONV_W = 4
LRU_C = 8.0
REL_BUCKETS = 32
REL_MAX_DIST = 128
N_EXPERTS = 8
TOP_K = 2
LN_EPS = 1e-5

LANES = 128
SUBLANES = 8
VMEM_LIMIT_BYTES = 56 * 1024 * 1024

KEY_TILE = 2 * Q_BLOCK
COUNT_TILE = 2 * KEY_TILE
HEADS_PER_DOT = 2
N_BIAS_LANES = 3
NORM_SLACK = 1.02
MAX_LOGIT_SPAN = 120.0
DOWN_SPLIT = 4
INT_MIN = -(2 ** 31)
NEG_BIG = -1e30
LOG2E = math.log2(math.e)

BF16 = jnp.bfloat16
F32 = jnp.float32
I32 = jnp.int32

_NT_DIMS = (((1,), (1,)), ((), ()))


def _params(n_axes):
    return pltpu.CompilerParams(
        dimension_semantics=("arbitrary",) * n_axes,
        vmem_limit_bytes=VMEM_LIMIT_BYTES)


def _mm_kernel(x_ref, w_ref, s_ref, o_ref):
    acc = jnp.dot(x_ref[...], w_ref[...], preferred_element_type=F32)
    o_ref[...] = (acc * s_ref[...]).astype(o_ref.dtype)


def _matmul(x, w, scale, out_dtype, tm, tn):
    m, k = x.shape
    n = w.shape[1]
    return pl.pallas_call(
        _mm_kernel,
        grid=(m // tm, n // tn),
        in_specs=[pl.BlockSpec((tm, k), lambda i, j: (i, 0)),
                  pl.BlockSpec((k, tn), lambda i, j: (0, j)),
                  pl.BlockSpec((1, tn), lambda i, j: (0, j))],
        out_specs=pl.BlockSpec((tm, tn), lambda i, j: (i, j)),
        out_shape=jax.ShapeDtypeStruct((m, n), out_dtype),
        compiler_params=_params(2),
        name="proj_in",
    )(x, w, scale)


def _fold_rows(x, op, rows):
    parts = [x[r * rows:(r + 1) * rows, :] for r in range(x.shape[0] // rows)]
    while len(parts) > 1:
        nxt = [op(parts[j], parts[j + 1]) for j in range(0, len(parts) - 1, 2)]
        if len(parts) % 2:
            nxt.append(parts[-1])
        parts = nxt
    return parts[0]


def _fold8(x, op):
    return _fold_rows(x, op, SUBLANES)


def _attn_kernel(zs_ref, q_ref, qi_ref, aux_ref, hsel_ref, nb_ref, o_ref,
                 kp_s, ki_s, vt_s, q_s, qi_s, skey_s, mb_s, l_s, acc_s, ot_s, kmax_s,
                 *, k_top, n_heads, pos_bits):
    i = pl.program_id(1)
    seq = zs_ref.shape[0]
    n_dots = n_heads // HEADS_PER_DOT
    dot_w = HEADS_PER_DOT * Q_BLOCK
    n_aug = N_BIAS_LANES * n_heads
    start = pl.multiple_of(i * Q_BLOCK, Q_BLOCK)
    n_att = (i + 2) // 2
    n_cnt = (n_att + 1) // 2

    @pl.when(i == 0)
    def _():
        kv = zs_ref[:, 0:2 * KV_DIM]
        col = lax.broadcasted_iota(I32, kv.shape, 1)
        ones = jnp.where(col < KV_DIM + n_aug, 1.0, 0.0)
        kp = jnp.where(col < KV_DIM, kv, ones).astype(BF16)
        kp_s[...] = kp
        k_sq = jnp.where(col < KV_DIM, jnp.square(kp.astype(F32)), 0.0)
        kmax_s[0] = jnp.sqrt(jnp.max(jnp.sum(k_sq, axis=1, keepdims=True)))
        vt_s[...] = kv.T[KV_DIM:2 * KV_DIM, :].astype(BF16)
        ki_s[...] = zs_ref[:, 2 * KV_DIM:2 * KV_DIM + IDX_DIM].astype(BF16)
        skey_s[...] = jnp.full(skey_s.shape, INT_MIN, I32)

    qf = q_ref[...].astype(F32)
    q_sq = jnp.dot(jnp.square(qf).astype(BF16), hsel_ref[...], preferred_element_type=F32)
    q_norm = jnp.sqrt(q_sq * NORM_SLACK)
    reach = q_norm * kmax_s[0]
    c_left = aux_ref[0:1, :] - (reach + aux_ref[1:2, :])
    safe = jnp.max(2.0 * reach + aux_ref[2:3, :]) <= MAX_LOGIT_SPAN
    terms = []
    for _ in range(N_BIAS_LANES):
        terms.append(c_left.astype(BF16).astype(F32))
        c_left = c_left - terms[-1]
    aug = jnp.concatenate([t[:, 0:n_heads] for t in terms]
                          + [jnp.zeros((Q_BLOCK, LANES - HEAD_DIM - n_aug), F32)], axis=1)
    aug_lane = lax.broadcasted_iota(I32, aug.shape, 1)

    for h in range(n_heads):
        rows = slice(h * Q_BLOCK, (h + 1) * Q_BLOCK)
        q_s[rows, 0:HEAD_DIM] = q_ref[:, h * HEAD_DIM:(h + 1) * HEAD_DIM]
        own = (aug_lane < n_aug) & ((aug_lane & (n_heads - 1)) == h)
        q_s[rows, HEAD_DIM:] = jnp.where(own, aug, 0.0).astype(BF16)
    for h in range(IDX_HEADS):
        qi_s[h * Q_BLOCK:(h + 1) * Q_BLOCK, :] = qi_ref[:, h * IDX_DIM:(h + 1) * IDX_DIM]

    w_t = zs_ref[pl.ds(start, Q_BLOCK), LANES:2 * LANES].T

    row = lax.broadcasted_iota(I32, (KEY_TILE, Q_BLOCK), 0)
    lane = lax.broadcasted_iota(I32, (KEY_TILE, Q_BLOCK), 1)
    row_cnt = lax.broadcasted_iota(I32, (COUNT_TILE, Q_BLOCK), 0)
    q_chunk = lax.shift_right_logical(start + lane, CHUNK_SHIFT)

    def score_tiles(g2, carry):
        for sub in range(COUNT_TILE // KEY_TILE):
            off = pl.multiple_of(g2 * COUNT_TILE + sub * KEY_TILE, KEY_TILE)
            ki_t = ki_s[pl.ds(off, KEY_TILE), :]
            acc = jnp.zeros((KEY_TILE, Q_BLOCK), F32)
            for c in range(IDX_HEADS // HEADS_PER_DOT):
                d = lax.dot_general(ki_t, qi_s[c * dot_w:(c + 1) * dot_w, :], _NT_DIMS,
                                    preferred_element_type=F32)
                for s in range(HEADS_PER_DOT):
                    h = c * HEADS_PER_DOT + s
                    acc = acc + (jnp.maximum(d[:, s * Q_BLOCK:(s + 1) * Q_BLOCK], 0.0)
                                 * w_t[IDX_DIM + h:IDX_DIM + h + 1, :])
            bits = lax.bitcast_convert_type(acc, I32)
            key = bits ^ ((bits >> 31) & jnp.int32(0x7FFFFFFF))
            k_chunk = lax.shift_right_logical(off + row, CHUNK_SHIFT)
            skey_s[pl.ds(off, KEY_TILE), :] = jnp.where(k_chunk <= q_chunk, key, INT_MIN)
        return carry

    lax.fori_loop(0, n_cnt, score_tiles, 0)

    def count(pred):
        def body(g, c):
            off = pl.multiple_of(g * COUNT_TILE, COUNT_TILE)
            hit = pred(skey_s[pl.ds(off, COUNT_TILE), :], off + row_cnt)
            return c + _fold8(jnp.where(hit, 1, 0).astype(I32), jnp.add)
        c = lax.fori_loop(0, n_cnt, body, jnp.zeros((SUBLANES, Q_BLOCK), I32))
        return jnp.sum(c, axis=0, keepdims=True)

    def bisect(it, carry):
        t, n_ge = carry
        cand = t ^ lax.shift_left(jnp.int32(1), 31 - it)
        c = count(lambda key, pos: key >= cand)
        return jnp.where(c >= k_top, cand, t), jnp.where(c >= k_top, c, n_ge)

    t, n_ge = lax.fori_loop(
        0, 32, bisect, (jnp.full((1, Q_BLOCK), INT_MIN, I32),
                        jnp.full((1, Q_BLOCK), n_cnt * COUNT_TILE, I32)))


    @pl.when(jnp.max(n_ge) > k_top)
    def _():
        n_gt = count(lambda key, pos: key > t)
        r_m1 = k_top - n_gt - 1

        def pos_search(it, j):
            cand = j | lax.shift_left(jnp.int32(1), pos_bits - 1 - it)
            c = count(lambda key, pos: (key == t) & (pos < cand))
            return jnp.where(c <= r_m1, cand, j)

        j_cut = lax.fori_loop(0, pos_bits, pos_search, jnp.zeros((1, Q_BLOCK), I32))

        def drop(g, carry):
            off = pl.multiple_of(g * KEY_TILE, KEY_TILE)
            key = skey_s[pl.ds(off, KEY_TILE), :]
            cut = (key == t) & ((off + row) > j_cut)
            skey_s[pl.ds(off, KEY_TILE), :] = jnp.where(cut, INT_MIN, key)
            return carry

        lax.fori_loop(0, n_att, drop, 0)

    t_sel = jnp.maximum(t, INT_MIN + 1)

    def mask_tile(g, carry):
        off = pl.multiple_of(g * KEY_TILE, KEY_TILE)
        mb_s[pl.ds(off, KEY_TILE), :] = jnp.where(
            skey_s[pl.ds(off, KEY_TILE), :] >= t_sel, 0.0, NEG_BIG)
        return carry

    lax.fori_loop(0, n_att, mask_tile, 0)

    n_near = jnp.where((i & 1) == 0, jnp.minimum(n_att, 2), 1)
    n_far = n_att - n_near
    acc_s[...] = jnp.zeros(acc_s.shape, F32)
    l8_zero = tuple(jnp.zeros((SUBLANES, Q_BLOCK), F32) for _ in range(n_heads))

    def shifted_logits(lg, s, h, mb, table):
        x = lg[:, s * Q_BLOCK:(s + 1) * Q_BLOCK] + mb
        return x if table is None else x + nb_ref[table, h]

    def single_pass():
        def tiles(specs, l8):
            staged = []
            for g, table in specs:
                off = pl.multiple_of(g * KEY_TILE, KEY_TILE)
                kp_t = kp_s[pl.ds(off, KEY_TILE), :]
                lgs = [lax.dot_general(kp_t, q_s[c * dot_w:(c + 1) * dot_w, :], _NT_DIMS,
                                       preferred_element_type=F32) for c in range(n_dots)]
                staged.append((vt_s[:, pl.ds(off, KEY_TILE)], mb_s[pl.ds(off, KEY_TILE), :],
                               table, lgs))
            out = list(l8)
            for vt_t, mb, table, lgs in staged:
                for c in range(n_dots):
                    ps = []
                    for s in range(HEADS_PER_DOT):
                        h = c * HEADS_PER_DOT + s
                        p = jnp.exp2(shifted_logits(lgs[c], s, h, mb, table))
                        out[h] = out[h] + _fold8(p, jnp.add)
                        ps.append(p.astype(BF16))
                    acc_s[c] += jnp.dot(vt_t, jnp.concatenate(ps, axis=1),
                                        preferred_element_type=F32)
            return tuple(out)

        n_pair = n_far // 2
        l8 = lax.fori_loop(0, n_pair, lambda p, l: tiles([(2 * p, None), (2 * p + 1, None)], l),
                           l8_zero)
        l8 = lax.fori_loop(2 * n_pair, n_far, lambda g, l: tiles([(g, None)], l), l8)
        return lax.fori_loop(n_far, n_att, lambda g, l: tiles([(g, 2 * g - i + 2)], l), l8)

    def two_pass():
        def logits_tile(g, m8, table):
            off = pl.multiple_of(g * KEY_TILE, KEY_TILE)
            kp_t = kp_s[pl.ds(off, KEY_TILE), :]
            mb = mb_s[pl.ds(off, KEY_TILE), :]
            out = []
            for c in range(n_dots):
                lg = lax.dot_general(kp_t, q_s[c * dot_w:(c + 1) * dot_w, :], _NT_DIMS,
                                     preferred_element_type=F32)
                for s in range(HEADS_PER_DOT):
                    h = c * HEADS_PER_DOT + s
                    x = shifted_logits(lg, s, h, mb, table)
                    l_s[h, pl.ds(off, KEY_TILE), :] = x
                    out.append(jnp.maximum(m8[h], _fold8(x, jnp.maximum)))
            return tuple(out)

        m8 = tuple(jnp.full((SUBLANES, Q_BLOCK), NEG_BIG, F32) for _ in range(n_heads))
        m8 = lax.fori_loop(0, n_far, lambda g, m: logits_tile(g, m, None), m8)
        m8 = lax.fori_loop(n_far, n_att, lambda g, m: logits_tile(g, m, 2 * g - i + 2), m8)
        m_row = [jnp.max(m, axis=0, keepdims=True) for m in m8]

        def pv_tile(g, l8):
            off = pl.multiple_of(g * KEY_TILE, KEY_TILE)
            vt_t = vt_s[:, pl.ds(off, KEY_TILE)]
            out = []
            for c in range(n_dots):
                ps = []
                for s in range(HEADS_PER_DOT):
                    h = c * HEADS_PER_DOT + s
                    p = jnp.exp2(l_s[h, pl.ds(off, KEY_TILE), :] - m_row[h])
                    out.append(l8[h] + _fold8(p, jnp.add))
                    ps.append(p.astype(BF16))
                acc_s[c] += jnp.dot(vt_t, jnp.concatenate(ps, axis=1), preferred_element_type=F32)
            return tuple(out)

        return lax.fori_loop(0, n_att, pv_tile, l8_zero)

    l8 = lax.cond(safe, single_pass, two_pass)

    for c in range(n_dots):
        acc = acc_s[c]
        for s in range(HEADS_PER_DOT):
            h = c * HEADS_PER_DOT + s
            denom = jnp.sum(l8[h], axis=0, keepdims=True)
            ot_s[h * HEAD_DIM:(h + 1) * HEAD_DIM, :] = acc[:, s * Q_BLOCK:(s + 1) * Q_BLOCK] / denom
    o_ref[...] = ot_s[...].T.astype(o_ref.dtype)


def _attention(zs, zb, near_bias, aux, batch, seq, n_heads, qi_block):
    d_q = n_heads * HEAD_DIM
    nqb = seq // Q_BLOCK
    k_top = min(INDEX_TOPK, seq // 4)
    assert seq % COUNT_TILE == 0 and n_heads % HEADS_PER_DOT == 0
    assert n_heads & (n_heads - 1) == 0 and N_BIAS_LANES * n_heads <= LANES - HEAD_DIM
    hsel = (np.arange(d_q)[:, None] // HEAD_DIM == np.arange(LANES)[None, :]).astype(np.float32)
    kern = functools.partial(_attn_kernel, k_top=k_top, n_heads=n_heads,
                             pos_bits=max(1, (seq - 1).bit_length()))
    return pl.pallas_call(
        kern,
        grid=(batch, nqb),
        in_specs=[
            pl.BlockSpec((seq, 2 * LANES), lambda b, i: (b, 0)),
            pl.BlockSpec((Q_BLOCK, d_q), lambda b, i: (b * nqb + i, 0)),
            pl.BlockSpec((Q_BLOCK, IDX_HEADS * IDX_DIM), lambda b, i: (b * nqb + i, qi_block)),
            pl.BlockSpec(aux.shape, lambda b, i: (0, 0)),
            pl.BlockSpec(hsel.shape, lambda b, i: (0, 0)),
            pl.BlockSpec(near_bias.shape, lambda b, i: (0, 0, 0, 0)),
        ],
        out_specs=pl.BlockSpec((Q_BLOCK, d_q), lambda b, i: (b * nqb + i, 0)),
        out_shape=jax.ShapeDtypeStruct((batch * seq, d_q), BF16),
        scratch_shapes=[
            pltpu.VMEM((seq, LANES), BF16),
            pltpu.VMEM((seq, IDX_DIM), BF16),
            pltpu.VMEM((KV_DIM, seq), BF16),
            pltpu.VMEM((n_heads * Q_BLOCK, LANES), BF16),
            pltpu.VMEM((IDX_HEADS * Q_BLOCK, IDX_DIM), BF16),
            pltpu.VMEM((seq, Q_BLOCK), I32),
            pltpu.VMEM((seq, Q_BLOCK), F32),
            pltpu.VMEM((n_heads, seq, Q_BLOCK), F32),
            pltpu.VMEM((n_heads // HEADS_PER_DOT, KV_DIM, HEADS_PER_DOT * Q_BLOCK), F32),
            pltpu.VMEM((d_q, Q_BLOCK), F32),
            pltpu.SMEM((1,), F32),
        ],
        compiler_params=_params(2),
        name="dsa_attention",
    )(zs, zb, zb, aux, jnp.asarray(hsel, BF16), near_bias)


def _t5_bucket(rel):
    half = REL_BUCKETS // 2
    max_exact = half // 2
    ret = jnp.where(rel > 0, half, 0)
    n = jnp.abs(rel)
    nf = jnp.maximum(n, 1).astype(F32)
    large = max_exact + (jnp.log(nf / max_exact) / math.log(REL_MAX_DIST / max_exact)
                         * (half - max_exact)).astype(I32)
    large = jnp.minimum(large, half - 1)
    return ret + jnp.where(n < max_exact, n, large)


def _bias_tables(rel_bias):
    n_heads = rel_bias.shape[1]
    bias2 = rel_bias.astype(F32) * LOG2E
    far = bias2[_t5_bucket(jnp.asarray(-2 * REL_MAX_DIST, I32))]
    b_max = jnp.max(bias2, axis=0)
    aux = jnp.zeros((SUBLANES, LANES), F32).at[0:3, :n_heads].set(
        jnp.stack([far, b_max, b_max - jnp.min(bias2, axis=0)]))
    rel_min = -2 * Q_BLOCK - (Q_BLOCK - 1)
    by_rel = bias2[_t5_bucket(jnp.arange(rel_min, KEY_TILE, dtype=I32))] - far[None, :]
    span = KEY_TILE + Q_BLOCK - 1
    tiles = []
    for d in range(3):
        w = lax.slice_in_dim(by_rel, d * Q_BLOCK, d * Q_BLOCK + span, axis=0)
        wrapped = jnp.tile(jnp.concatenate([w, jnp.zeros((1, n_heads), F32)]), (Q_BLOCK, 1))
        rows = wrapped[:Q_BLOCK * span].reshape(Q_BLOCK, span, n_heads)
        tiles.append(rows[:, Q_BLOCK - 1:Q_BLOCK - 1 + KEY_TILE, :])
    near = jnp.transpose(jnp.stack(tiles), (0, 3, 2, 1))
    return near, aux


def _one_minus_sq(a, log_a):
    y = 2.0 * log_a
    series = -y * (1.0 + y * (1.0 / 2 + y * (1.0 / 6)))
    return jnp.where(y > -0.03, series, 1.0 - a * a)


def _rglru_kernel(xr_ref, yr_ref, sh_ref, cw_ref, cb_ref, wa_ref, ba_ref, wx_ref, bx_ref, lam_ref,
                  o_ref, ext_s, xc_s, a_s, u_s, h_s, *, ts, bw):
    j = pl.program_id(1)
    width = xc_s.shape[1]

    @pl.when(j == 0)
    def _():
        ext_s[0:SUBLANES, :] = jnp.zeros((SUBLANES, width), F32)
        h_s[...] = jnp.zeros(h_s.shape, F32)

    x16 = xr_ref[...]
    x = x16.astype(F32)
    taps = [jnp.dot(sh_ref[k], x16, preferred_element_type=F32) for k in range(CONV_W - 1)]
    xc = taps[0] * cw_ref[0:1, :]
    for k in range(1, CONV_W - 1):
        xc = xc + taps[k] * cw_ref[k:k + 1, :]
    xc_s[...] = xc + x * cw_ref[CONV_W - 1:CONV_W, :] + cb_ref[...]
    ext_s[SUBLANES:2 * SUBLANES, :] = x[0:SUBLANES, :]
    head = ext_s[SUBLANES - 3:2 * SUBLANES - 3, :] * cw_ref[0:1, :]
    for k in range(1, CONV_W):
        head = head + ext_s[SUBLANES - 3 + k:2 * SUBLANES - 3 + k, :] * cw_ref[k:k + 1, :]
    xc_s[0:SUBLANES, :] = head + cb_ref[...]
    ext_s[0:SUBLANES, :] = x[ts - SUBLANES:ts, :]

    sp = jax.nn.softplus(-lam_ref[...])
    for n in range(RNN_BLOCKS):
        cs = slice(n * bw, (n + 1) * bw)
        xb = xc_s[:, cs]
        xb16 = xb.astype(BF16)
        r = jax.nn.sigmoid(jnp.dot(xb16, wa_ref[n], preferred_element_type=F32) + ba_ref[:, cs])
        g = jax.nn.sigmoid(jnp.dot(xb16, wx_ref[n], preferred_element_type=F32) + bx_ref[:, cs])
        log_a = -LRU_C * r * sp[:, cs]
        a = jnp.exp(log_a)
        a_s[:, cs] = a
        u_s[:, cs] = jnp.sqrt(_one_minus_sq(a, log_a)) * (g * xb)

    row8 = lax.broadcasted_iota(I32, (SUBLANES, width), 0)

    def scan_rows(blk, h_prev):
        r0 = pl.multiple_of(blk * SUBLANES, SUBLANES)
        a_cum = a_s[pl.ds(r0, SUBLANES), :]
        u_cum = u_s[pl.ds(r0, SUBLANES), :]
        d = 1
        while d < SUBLANES:
            keep = row8 >= d
            u_cum = jnp.where(keep, a_cum * pltpu.roll(u_cum, d, 0) + u_cum, u_cum)
            a_cum = jnp.where(keep, a_cum * pltpu.roll(a_cum, d, 0), a_cum)
            d *= 2
        h = a_cum * h_prev + u_cum
        u_s[pl.ds(r0, SUBLANES), :] = h
        return h[SUBLANES - 1:SUBLANES, :]

    h_s[...] = lax.fori_loop(0, ts // SUBLANES, scan_rows, h_s[...], unroll=4)
    o_ref[...] = (u_s[...] * jax.nn.gelu(yr_ref[...].astype(F32))).astype(o_ref.dtype)


def _rglru(zb, conv_w, conv_b, wa, ba, wx, bx, lam, batch, seq, width, xr_block, yr_block, ts):
    nts = seq // ts
    bw = width // RNN_BLOCKS
    row = lambda v: v.reshape(1, width)
    full = lambda shape: pl.BlockSpec(shape, lambda b, j: (0,) * len(shape))
    t_out, t_in = np.arange(ts)[:, None], np.arange(ts)[None, :]
    shifts = np.stack([(t_out - t_in == CONV_W - 1 - k) for k in range(CONV_W - 1)]).astype(np.float32)
    return pl.pallas_call(
        functools.partial(_rglru_kernel, ts=ts, bw=bw),
        grid=(batch, nts),
        in_specs=[pl.BlockSpec((ts, width), lambda b, j: (b * nts + j, xr_block)),
                  pl.BlockSpec((ts, width), lambda b, j: (b * nts + j, yr_block)),
                  full((CONV_W - 1, ts, ts)),
                  full((CONV_W, width)), full((1, width)),
                  full((RNN_BLOCKS, bw, bw)), full((1, width)),
                  full((RNN_BLOCKS, bw, bw)), full((1, width)), full((1, width))],
        out_specs=pl.BlockSpec((ts, width), lambda b, j: (b * nts + j, 0)),
        out_shape=jax.ShapeDtypeStruct((batch * seq, width), BF16),
        scratch_shapes=[pltpu.VMEM((2 * SUBLANES, width), F32),
                        pltpu.VMEM((ts, width), F32),
                        pltpu.VMEM((ts, width), F32),
                        pltpu.VMEM((ts, width), F32),
                        pltpu.VMEM((1, width), F32)],
        compiler_params=_params(2),
        name="rglru",
    )(zb, zb, jnp.asarray(shifts, BF16), conv_w, row(conv_b), wa.astype(BF16), row(ba),
      wx.astype(BF16), row(bx), row(lam))


def _layer_norm(y, g, b):
    mu = jnp.mean(y, axis=-1, keepdims=True)
    var = jnp.mean(jnp.square(y - mu), axis=-1, keepdims=True)
    return (y - mu) * lax.rsqrt(var + LN_EPS) * g + b


def _top2(lg):
    lane = lax.broadcasted_iota(I32, lg.shape, 1)
    v1 = jnp.max(lg, axis=1, keepdims=True)
    i1 = jnp.min(jnp.where(lg == v1, lane, LANES), axis=1, keepdims=True)
    lg2 = jnp.where(lane == i1, -jnp.inf, lg)
    v2 = jnp.max(lg2, axis=1, keepdims=True)
    i2 = jnp.min(jnp.where(lg2 == v2, lane, LANES), axis=1, keepdims=True)
    e2 = jnp.exp(v2 - v1)
    denom = 1.0 + e2
    idx = jnp.where(lane == 0, i1, jnp.where(lane == 1, i2, 0))
    gates = jnp.where(lane == 0, 1.0 / denom, jnp.where(lane == 1, e2 / denom, 0.0))
    return idx, gates


def _merge_kernel(oa_ref, or_ref, ga_ref, gr_ref, x_ref, wba_ref, wbr_ref, wo_ref, g_ref, b_ref,
                  *rest, alpha, routed):
    a = jnp.dot(oa_ref[...], wba_ref[...], preferred_element_type=F32)
    r = jnp.dot(or_ref[...], wbr_ref[...], preferred_element_type=F32)
    merged = (jax.nn.sigmoid(ga_ref[...].astype(F32)) * a
              + jax.nn.sigmoid(gr_ref[...].astype(F32)) * r)
    mix = jnp.dot(merged.astype(BF16), wo_ref[...], preferred_element_type=F32)
    x1 = _layer_norm(alpha * x_ref[...] + mix, g_ref[...], b_ref[...])
    x1b = x1.astype(BF16)
    if routed:
        rw_ref, rb_ref, x1_ref, x1b_ref, idx_ref, gate_ref = rest
        idx_ref[...], gate_ref[...] = _top2(
            jnp.dot(x1b, rw_ref[...], preferred_element_type=F32) + rb_ref[...])
    else:
        x1_ref, x1b_ref = rest
    x1_ref[...] = x1
    x1b_ref[...] = x1b


def _merge_ln(oa, orn, zb, x, wba, wbr, wo, g, b, alpha, ga_block, gr_block, tm, router=None):
    m, d = x.shape
    tok = lambda blk: pl.BlockSpec((tm, d), lambda i: (i, blk))
    full = lambda shape: pl.BlockSpec(shape, lambda i: (0,) * len(shape))
    lanes = pl.BlockSpec((tm, LANES), lambda i: (i, 0))
    args = [oa, orn, zb, zb, x, wba, wbr, wo, g.reshape(1, d), b.reshape(1, d)]
    in_specs = [tok(0), tok(0), tok(ga_block), tok(gr_block), tok(0),
                full((d, d)), full((d, d)), full((d, d)), full((1, d)), full((1, d))]
    out_specs = [tok(0), tok(0)]
    out_shape = [jax.ShapeDtypeStruct((m, d), F32), jax.ShapeDtypeStruct((m, d), BF16)]
    if router is not None:
        rw = jnp.zeros((d, LANES), BF16).at[:, :N_EXPERTS].set(router[0].astype(BF16))
        rb = jnp.full((1, LANES), -jnp.inf, F32).at[0, :N_EXPERTS].set(router[1].astype(F32))
        args += [rw, rb]
        in_specs += [full((d, LANES)), full((1, LANES))]
        out_specs += [lanes, lanes]
        out_shape += [jax.ShapeDtypeStruct((m, LANES), I32), jax.ShapeDtypeStruct((m, LANES), F32)]
    return pl.pallas_call(
        functools.partial(_merge_kernel, alpha=alpha, routed=router is not None),
        grid=(m // tm,),
        in_specs=in_specs, out_specs=out_specs, out_shape=out_shape,
        compiler_params=_params(1),
        name="merge_ln1",
    )(*args)


def _ple_kernel(*refs, alpha, routed):
    if routed:
        (x1_ref, x1b_ref, p_ref, pg_ref, pp_ref, g_ref, b_ref,
         ya_ref, yb_ref, gt_ref, x2_ref, x2b_ref) = refs
        f = (gt_ref[:, 0:1] * ya_ref[...].astype(F32)
             + gt_ref[:, 1:2] * yb_ref[...].astype(F32))
    else:
        x1_ref, x1b_ref, p_ref, pg_ref, pp_ref, g_ref, b_ref, f_ref, x2_ref, x2b_ref = refs
        f = f_ref[...].astype(F32)
    gate = jax.nn.sigmoid(jnp.dot(x1b_ref[...], pg_ref[...], preferred_element_type=F32))
    proj = jnp.dot(p_ref[...].astype(BF16), pp_ref[...], preferred_element_type=F32)
    x2 = _layer_norm(alpha * x1_ref[...] + f + gate * proj, g_ref[...], b_ref[...])
    x2_ref[...] = x2
    x2b_ref[...] = x2.astype(BF16)


def _ple_ln(x1, x1b, p_all, layer, pg, pp, g, b, f_parts, alpha, tm):
    m, d = x1.shape
    pd = p_all.shape[1]
    nt = m // tm
    tok = pl.BlockSpec((tm, d), lambda i: (i, 0))
    full = lambda shape: pl.BlockSpec(shape, lambda i: (0,) * len(shape))
    routed = len(f_parts) == 3
    f_specs = [tok, tok, pl.BlockSpec((tm, LANES), lambda i: (i, 0))] if routed else [tok]
    return pl.pallas_call(
        functools.partial(_ple_kernel, alpha=alpha, routed=routed),
        grid=(nt,),
        in_specs=[tok, tok, pl.BlockSpec((tm, pd), lambda i: (layer * nt + i, 0)),
                  full((d, d)), full((pd, d)), full((1, d)), full((1, d))] + f_specs,
        out_specs=[tok, tok],
        out_shape=[jax.ShapeDtypeStruct((m, d), F32), jax.ShapeDtypeStruct((m, d), BF16)],
        compiler_params=_params(1),
        name="ple_ln2",
    )(x1, x1b, p_all, pg, pp, g.reshape(1, d), b.reshape(1, d), *f_parts)


def _ffn_kernel(te_ref, nt_ref, x_ref, wg_ref, wu_ref, wd_ref, o_ref, acc_s):
    t = pl.program_id(0)
    j = pl.program_id(1)
    last = pl.num_programs(1) - 1
    live = t < nt_ref[0]

    @pl.when(live)
    def _():
        wg = wg_ref[0].astype(BF16)
        wu = wu_ref[0].astype(BF16)
        wd = wd_ref[0].astype(BF16)
        x = x_ref[...]
        gate = jnp.dot(x, wg, preferred_element_type=F32)
        up = jnp.dot(x, wu, preferred_element_type=F32)
        hid = (jax.nn.silu(gate) * up).astype(BF16)
        cw = wd.shape[1] // DOWN_SPLIT
        for c in range(DOWN_SPLIT):
            cols = slice(c * cw, (c + 1) * cw)
            part = jnp.dot(hid, wd[:, cols], preferred_element_type=F32)
            acc_s[:, cols] = jnp.where(j == 0, part, acc_s[:, cols] + part)

    @pl.when(j == last)
    def _():
        o_ref[...] = jnp.where(live, acc_s[...], 0.0).astype(o_ref.dtype)


def _ffn(x, tile_expert, n_live, wg, wu, wd, tm, tf):
    rows, d = x.shape
    f = wg.shape[2]
    fcol = lambda t, j, te, nt: jnp.where(t < nt[0], j, 0)
    grid_spec = pltpu.PrefetchScalarGridSpec(
        num_scalar_prefetch=2,
        grid=(rows // tm, f // tf),
        in_specs=[pl.BlockSpec((tm, d), lambda t, j, te, nt: (t, 0)),
                  pl.BlockSpec((1, d, tf), lambda t, j, te, nt: (te[t], 0, fcol(t, j, te, nt))),
                  pl.BlockSpec((1, d, tf), lambda t, j, te, nt: (te[t], 0, fcol(t, j, te, nt))),
                  pl.BlockSpec((1, tf, d), lambda t, j, te, nt: (te[t], fcol(t, j, te, nt), 0))],
        out_specs=pl.BlockSpec((tm, d), lambda t, j, te, nt: (t, 0)),
        scratch_shapes=[pltpu.VMEM((tm, d), F32)])
    return pl.pallas_call(
        _ffn_kernel, grid_spec=grid_spec,
        out_shape=jax.ShapeDtypeStruct((rows, d), BF16),
        compiler_params=_params(2),
        name="swiglu_ffn",
    )(tile_expert, n_live, x, wg, wu, wd)


def _dispatch_plan(top_i, tm):
    m = top_i.shape[0]
    n_assign = m * TOP_K
    e_flat = top_i.reshape(n_assign)
    onehot = (e_flat[:, None] == jnp.arange(N_EXPERTS, dtype=I32)[None, :]).astype(I32)
    csum = jnp.cumsum(onehot, axis=0)
    counts = csum[-1]
    rank = jnp.sum(csum * onehot, axis=1) - 1
    padded = ((counts + tm - 1) // tm) * tm
    ends = jnp.cumsum(padded)
    pos = (ends - padded)[e_flat] + rank
    n_rows = n_assign + N_EXPERTS * tm
    src = jnp.zeros((n_rows,), I32).at[pos].set(jnp.arange(n_assign, dtype=I32) // TOP_K)
    tile_start = jnp.arange(n_rows // tm, dtype=I32) * tm
    tile_expert = jnp.minimum(jnp.searchsorted(ends, tile_start, side="right"),
                              N_EXPERTS - 1).astype(I32)
    n_live = (ends[-1] // tm).astype(I32).reshape(1)
    return src, pos.reshape(m, TOP_K), tile_expert, n_live


def _pick(n, candidates):
    for c in candidates:
        if n % c == 0:
            return c
    return n


def _forward(x, p, w_in, conv_w, conv_b, lru_wa, lru_ba, lru_wx, lru_bx, lru_lam, w_br_attn,
             w_br_rnn, w_o, rel_bias, ln1_g, ln1_b, ffn_w_gate, ffn_w_up, ffn_w_down, moe_router,
             moe_router_b, moe_w_gate, moe_w_up, moe_w_down, ple_w_gate, ple_w_proj, ln2_g, ln2_b):
    batch, seq, d = x.shape
    depth = w_in.shape[0]
    m = batch * seq
    n_heads = d // HEAD_DIM
    alpha = (2 * depth) ** 0.25
    d_q, d_qi = n_heads * HEAD_DIM, IDX_HEADS * IDX_DIM
    assert d_q == d and w_br_rnn.shape[1] == d and (d_q + 4 * d) % d_qi == 0

    sizes = (d_q, KV_DIM, KV_DIM, d_qi, IDX_DIM, IDX_HEADS, d, d, d, d)
    offs = np.concatenate([[0], np.cumsum(sizes)])
    col = lambda w, k: w[:, offs[k]:offs[k + 1]]
    q_scale = HEAD_DIM ** -0.5 * LOG2E
    big_scale = jnp.concatenate([jnp.full((1, d_q), q_scale, F32),
                                 jnp.ones((1, 4 * d + d_qi), F32)], axis=1)
    small_scale = jnp.ones((1, 2 * LANES), F32)
    xr_block, yr_block, ga_block, gr_block = 1, 2, 3, 4
    qi_block = (d_q + 4 * d) // d_qi

    near_bias, attn_aux = _bias_tables(rel_bias)

    tm = _pick(m, (1024, 512, 256, 128))
    tn_big = _pick(d_q + 4 * d + d_qi, (512, 256, 128))
    ts = _pick(seq, (256, 128))
    tm_res = _pick(m, (512, 256, 128))

    xf = x.reshape(m, d)
    xb = xf.astype(BF16)
    p_all = p.reshape(depth * m, p.shape[-1])

    for i in range(depth):
        w = w_in[i]
        w_big = jnp.concatenate([col(w, 0), col(w, 6), col(w, 7), col(w, 8), col(w, 9), col(w, 3)],
                                axis=1).astype(BF16)
        w_small = jnp.concatenate(
            [col(w, 1), col(w, 2), col(w, 4), col(w, 5),
             jnp.zeros((d, 2 * LANES - 2 * KV_DIM - IDX_DIM - IDX_HEADS), w.dtype)], axis=1).astype(BF16)
        zb = _matmul(xb, w_big, big_scale, BF16, _pick(m, (2 * tm, tm)), tn_big)
        zs = _matmul(xb, w_small, small_scale, F32, tm, 2 * LANES)

        o_attn = _attention(zs, zb, near_bias, attn_aux, batch, seq, n_heads, qi_block)
        o_rnn = _rglru(zb, conv_w[i], conv_b[i], lru_wa[i], lru_ba[i], lru_wx[i], lru_bx[i],
                       lru_lam[i], batch, seq, d, xr_block, yr_block, ts)
        j = i // 2
        router = None if i % 2 == 0 else (moe_router[j], moe_router_b[j])
        merged = _merge_ln(o_attn, o_rnn, zb, xf, w_br_attn[i].astype(BF16),
                           w_br_rnn[i].astype(BF16), w_o[i].astype(BF16), ln1_g[i], ln1_b[i],
                           alpha, ga_block, gr_block, tm_res, router)
        x1, x1b = merged[0], merged[1]

        if i % 2 == 0:
            f_dim = ffn_w_gate.shape[-1]
            tf = _pick(f_dim, (512, f_dim // 2))
            n_tiles = m // tm
            f_parts = [_ffn(x1b, jnp.zeros((n_tiles,), I32), jnp.full((1,), n_tiles, I32),
                            ffn_w_gate[j][None].astype(BF16), ffn_w_up[j][None].astype(BF16),
                            ffn_w_down[j][None].astype(BF16), tm, tf)]
        else:
            f_dim = moe_w_gate.shape[-1]
            tf = _pick(f_dim, (512, f_dim // 2))
            idx, gates = merged[2], merged[3]
            src, pos, tile_expert, n_live = _dispatch_plan(idx[:, :TOP_K], tm)
            all_experts = lambda w: w.reshape((-1,) + w.shape[2:])
            y = _ffn(x1b[src], tile_expert + j * N_EXPERTS, n_live, all_experts(moe_w_gate),
                     all_experts(moe_w_up), all_experts(moe_w_down), tm, tf)
            f_parts = [y[pos[:, 0]], y[pos[:, 1]], gates]

        xf, xb = _ple_ln(x1, x1b, p_all, i, ple_w_gate[i].astype(BF16), ple_w_proj[i].astype(BF16),
                         ln2_g[i], ln2_b[i], f_parts, alpha, tm_res)

    return xf.reshape(batch, seq, d)


@jax.jit
def kernel(x, p, w_in, conv_w, conv_b, lru_wa, lru_ba, lru_wx, lru_bx, lru_lam, w_br_attn, w_br_rnn,
           w_o, rel_bias, ln1_g, ln1_b, ffn_w_gate, ffn_w_up, ffn_w_down, moe_router, moe_router_b,
           moe_w_gate, moe_w_up, moe_w_down, ple_w_gate, ple_w_proj, ln2_g, ln2_b):
    return _forward(x, p, w_in, conv_w, conv_b, lru_wa, lru_ba, lru_wx, lru_bx, lru_lam, w_br_attn,
                    w_br_rnn, w_o, rel_bias, ln1_g, ln1_b, ffn_w_gate, ffn_w_up, ffn_w_down,
                    moe_router, moe_router_b, moe_w_gate, moe_w_up, moe_w_down, ple_w_gate,
                    ple_w_proj, ln2_g, ln2_b)
```

```python
import functools
import math

import jax
import jax.numpy as jnp
import numpy as np
from jax import lax
from jax.experimental import pallas as pl
from jax.experimental.pallas import tpu as pltpu

CHUNK = 64
CHUNK_SHIFT = CHUNK.bit_length() - 1
Q_BLOCK = 128
HEAD_DIM = 64
KV_DIM = 64
IDX_HEADS = 8
IDX_DIM = 64
INDEX_TOPK = 256
RNN_BLOCKS = 8
CONV_W = 4
LRU_C = 8.0
REL_BUCKETS = 32
REL_MAX_DIST = 128
N_EXPERTS = 8
TOP_K = 2
LN_EPS = 1e-5

LANES = 128
SUBLANES = 8
VMEM_LIMIT_BYTES = 56 * 1024 * 1024

KEY_TILE = 2 * Q_BLOCK
COUNT_TILE = 2 * KEY_TILE
HEADS_PER_DOT = 2
N_BIAS_LANES = 3
NORM_SLACK = 1.02
MAX_LOGIT_SPAN = 120.0
DOWN_SPLIT = 4
INT_MIN = -(2 ** 31)
NEG_BIG = -1e30
LOG2E = math.log2(math.e)

BF16 = jnp.bfloat16
F32 = jnp.float32
I32 = jnp.int32

_NT_DIMS = (((1,), (1,)), ((), ()))


def _params(n_axes):
    return pltpu.CompilerParams(
        dimension_semantics=("arbitrary",) * n_axes,
        vmem_limit_bytes=VMEM_LIMIT_BYTES)


def _mm_kernel(x_ref, w_ref, s_ref, o_ref):
    acc = jnp.dot(x_ref[...], w_ref[...], preferred_element_type=F32)
    o_ref[...] = (acc * s_ref[...]).astype(o_ref.dtype)


def _matmul(x, w, scale, out_dtype, tm, tn):
    m, k = x.shape
    n = w.shape[1]
    return pl.pallas_call(
        _mm_kernel,
        grid=(m // tm, n // tn),
        in_specs=[pl.BlockSpec((tm, k), lambda i, j: (i, 0)),
                  pl.BlockSpec((k, tn), lambda i, j: (0, j)),
                  pl.BlockSpec((1, tn), lambda i, j: (0, j))],
        out_specs=pl.BlockSpec((tm, tn), lambda i, j: (i, j)),
        out_shape=jax.ShapeDtypeStruct((m, n), out_dtype),
        compiler_params=_params(2),
        name="proj_in",
    )(x, w, scale)


def _fold_rows(x, op, rows):
    parts = [x[r * rows:(r + 1) * rows, :] for r in range(x.shape[0] // rows)]
    while len(parts) > 1:
        nxt = [op(parts[j], parts[j + 1]) for j in range(0, len(parts) - 1, 2)]
        if len(parts) % 2:
            nxt.append(parts[-1])
        parts = nxt
    return parts[0]


def _fold8(x, op):
    return _fold_rows(x, op, SUBLANES)


def _attn_kernel(zs_ref, q_ref, qi_ref, aux_ref, hsel_ref, nb_ref, o_ref,
                 kp_s, ki_s, vt_s, q_s, qi_s, skey_s, mb_s, l_s, acc_s, ot_s, kmax_s,
                 *, k_top, n_heads, pos_bits):
    i = pl.program_id(1)
    seq = zs_ref.shape[0]
    n_dots = n_heads // HEADS_PER_DOT
    dot_w = HEADS_PER_DOT * Q_BLOCK
    n_aug = N_BIAS_LANES * n_heads
    start = pl.multiple_of(i * Q_BLOCK, Q_BLOCK)
    n_att = (i + 2) // 2
    n_cnt = (n_att + 1) // 2

    @pl.when(i == 0)
    def _():
        kv = zs_ref[:, 0:2 * KV_DIM]
        col = lax.broadcasted_iota(I32, kv.shape, 1)
        ones = jnp.where(col < KV_DIM + n_aug, 1.0, 0.0)
        kp = jnp.where(col < KV_DIM, kv, ones).astype(BF16)
        kp_s[...] = kp
        k_sq = jnp.where(col < KV_DIM, jnp.square(kp.astype(F32)), 0.0)
        kmax_s[0] = jnp.sqrt(jnp.max(jnp.sum(k_sq, axis=1, keepdims=True)))
        vt_s[...] = kv.T[KV_DIM:2 * KV_DIM, :].astype(BF16)
        ki_s[...] = zs_ref[:, 2 * KV_DIM:2 * KV_DIM + IDX_DIM].astype(BF16)
        skey_s[...] = jnp.full(skey_s.shape, INT_MIN, I32)

    qf = q_ref[...].astype(F32)
    q_sq = jnp.dot(jnp.square(qf).astype(BF16), hsel_ref[...], preferred_element_type=F32)
    q_norm = jnp.sqrt(q_sq * NORM_SLACK)
    reach = q_norm * kmax_s[0]
    c_left = aux_ref[0:1, :] - (reach + aux_ref[1:2, :])
    safe = jnp.max(2.0 * reach + aux_ref[2:3, :]) <= MAX_LOGIT_SPAN
    terms = []
    for _ in range(N_BIAS_LANES):
        terms.append(c_left.astype(BF16).astype(F32))
        c_left = c_left - terms[-1]
    aug = jnp.concatenate([t[:, 0:n_heads] for t in terms]
                          + [jnp.zeros((Q_BLOCK, LANES - HEAD_DIM - n_aug), F32)], axis=1)
    aug_lane = lax.broadcasted_iota(I32, aug.shape, 1)

    for h in range(n_heads):
        rows = slice(h * Q_BLOCK, (h + 1) * Q_BLOCK)
        q_s[rows, 0:HEAD_DIM] = q_ref[:, h * HEAD_DIM:(h + 1) * HEAD_DIM]
        own = (aug_lane < n_aug) & ((aug_lane & (n_heads - 1)) == h)
        q_s[rows, HEAD_DIM:] = jnp.where(own, aug, 0.0).astype(BF16)
    for h in range(IDX_HEADS):
        qi_s[h * Q_BLOCK:(h + 1) * Q_BLOCK, :] = qi_ref[:, h * IDX_DIM:(h + 1) * IDX_DIM]

    w_t = zs_ref[pl.ds(start, Q_BLOCK), LANES:2 * LANES].T

    row = lax.broadcasted_iota(I32, (KEY_TILE, Q_BLOCK), 0)
    lane = lax.broadcasted_iota(I32, (KEY_TILE, Q_BLOCK), 1)
    row_cnt = lax.broadcasted_iota(I32, (COUNT_TILE, Q_BLOCK), 0)
    q_chunk = lax.shift_right_logical(start + lane, CHUNK_SHIFT)

    def score_tiles(g2, carry):
        for sub in range(COUNT_TILE // KEY_TILE):
            off = pl.multiple_of(g2 * COUNT_TILE + sub * KEY_TILE, KEY_TILE)
            ki_t = ki_s[pl.ds(off, KEY_TILE), :]
            acc = jnp.zeros((KEY_TILE, Q_BLOCK), F32)
            for c in range(IDX_HEADS // HEADS_PER_DOT):
                d = lax.dot_general(ki_t, qi_s[c * dot_w:(c + 1) * dot_w, :], _NT_DIMS,
                                    preferred_element_type=F32)
                for s in range(HEADS_PER_DOT):
                    h = c * HEADS_PER_DOT + s
                    acc = acc + (jnp.maximum(d[:, s * Q_BLOCK:(s + 1) * Q_BLOCK], 0.0)
                                 * w_t[IDX_DIM + h:IDX_DIM + h + 1, :])
            bits = lax.bitcast_convert_type(acc, I32)
            key = bits ^ ((bits >> 31) & jnp.int32(0x7FFFFFFF))
            k_chunk = lax.shift_right_logical(off + row, CHUNK_SHIFT)
            skey_s[pl.ds(off, KEY_TILE), :] = jnp.where(k_chunk <= q_chunk, key, INT_MIN)
        return carry

    lax.fori_loop(0, n_cnt, score_tiles, 0)

    def count(pred):
        def body(g, c):
            off = pl.multiple_of(g * COUNT_TILE, COUNT_TILE)
            hit = pred(skey_s[pl.ds(off, COUNT_TILE), :], off + row_cnt)
            return c + _fold8(jnp.where(hit, 1, 0).astype(I32), jnp.add)
        c = lax.fori_loop(0, n_cnt, body, jnp.zeros((SUBLANES, Q_BLOCK), I32))
        return jnp.sum(c, axis=0, keepdims=True)

    def bisect(it, carry):
        t, n_ge = carry
        cand = t ^ lax.shift_left(jnp.int32(1), 31 - it)
        c = count(lambda key, pos: key >= cand)
        return jnp.where(c >= k_top, cand, t), jnp.where(c >= k_top, c, n_ge)

    t, n_ge = lax.fori_loop(
        0, 32, bisect, (jnp.full((1, Q_BLOCK), INT_MIN, I32),
                        jnp.full((1, Q_BLOCK), n_cnt * COUNT_TILE, I32)))


    @pl.when(jnp.max(n_ge) > k_top)
    def _():
        n_gt = count(lambda key, pos: key > t)
        r_m1 = k_top - n_gt - 1

        def pos_search(it, j):
            cand = j | lax.shift_left(jnp.int32(1), pos_bits - 1 - it)
            c = count(lambda key, pos: (key == t) & (pos < cand))
            return jnp.where(c <= r_m1, cand, j)

        j_cut = lax.fori_loop(0, pos_bits, pos_search, jnp.zeros((1, Q_BLOCK), I32))

        def drop(g, carry):
            off = pl.multiple_of(g * KEY_TILE, KEY_TILE)
            key = skey_s[pl.ds(off, KEY_TILE), :]
            cut = (key == t) & ((off + row) > j_cut)
            skey_s[pl.ds(off, KEY_TILE), :] = jnp.where(cut, INT_MIN, key)
            return carry

        lax.fori_loop(0, n_att, drop, 0)

    t_sel = jnp.maximum(t, INT_MIN + 1)

    def mask_tile(g, carry):
        off = pl.multiple_of(g * KEY_TILE, KEY_TILE)
        mb_s[pl.ds(off, KEY_TILE), :] = jnp.where(
            skey_s[pl.ds(off, KEY_TILE), :] >= t_sel, 0.0, NEG_BIG)
        return carry

    lax.fori_loop(0, n_att, mask_tile, 0)

    n_near = jnp.where((i & 1) == 0, jnp.minimum(n_att, 2), 1)
    n_far = n_att - n_near
    acc_s[...] = jnp.zeros(acc_s.shape, F32)
    l8_zero = tuple(jnp.zeros((SUBLANES, Q_BLOCK), F32) for _ in range(n_heads))

    def shifted_logits(lg, s, h, mb, table):
        x = lg[:, s * Q_BLOCK:(s + 1) * Q_BLOCK] + mb
        return x if table is None else x + nb_ref[table, h]

    def single_pass():
        def tiles(specs, l8):
            staged = []
            for g, table in specs:
                off = pl.multiple_of(g * KEY_TILE, KEY_TILE)
                kp_t = kp_s[pl.ds(off, KEY_TILE), :]
                lgs = [lax.dot_general(kp_t, q_s[c * dot_w:(c + 1) * dot_w, :], _NT_DIMS,
                                       preferred_element_type=F32) for c in range(n_dots)]
                staged.append((vt_s[:, pl.ds(off, KEY_TILE)], mb_s[pl.ds(off, KEY_TILE), :],
                               table, lgs))
            out = list(l8)
            for vt_t, mb, table, lgs in staged:
                for c in range(n_dots):
                    ps = []
                    for s in range(HEADS_PER_DOT):
                        h = c * HEADS_PER_DOT + s
                        p = jnp.exp2(shifted_logits(lgs[c], s, h, mb, table))
                        out[h] = out[h] + _fold8(p, jnp.add)
                        ps.append(p.astype(BF16))
                    acc_s[c] += jnp.dot(vt_t, jnp.concatenate(ps, axis=1),
                                        preferred_element_type=F32)
            return tuple(out)

        n_pair = n_far // 2
        l8 = lax.fori_loop(0, n_pair, lambda p, l: tiles([(2 * p, None), (2 * p + 1, None)], l),
                           l8_zero)
        l8 = lax.fori_loop(2 * n_pair, n_far, lambda g, l: tiles([(g, None)], l), l8)
        return lax.fori_loop(n_far, n_att, lambda g, l: tiles([(g, 2 * g - i + 2)], l), l8)

    def two_pass():
        def logits_tile(g, m8, table):
            off = pl.multiple_of(g * KEY_TILE, KEY_TILE)
            kp_t = kp_s[pl.ds(off, KEY_TILE), :]
            mb = mb_s[pl.ds(off, KEY_TILE), :]
            out = []
            for c in range(n_dots):
                lg = lax.dot_general(kp_t, q_s[c * dot_w:(c + 1) * dot_w, :], _NT_DIMS,
                                     preferred_element_type=F32)
                for s in range(HEADS_PER_DOT):
                    h = c * HEADS_PER_DOT + s
                    x = shifted_logits(lg, s, h, mb, table)
                    l_s[h, pl.ds(off, KEY_TILE), :] = x
                    out.append(jnp.maximum(m8[h], _fold8(x, jnp.maximum)))
            return tuple(out)

        m8 = tuple(jnp.full((SUBLANES, Q_BLOCK), NEG_BIG, F32) for _ in range(n_heads))
        m8 = lax.fori_loop(0, n_far, lambda g, m: logits_tile(g, m, None), m8)
        m8 = lax.fori_loop(n_far, n_att, lambda g, m: logits_tile(g, m, 2 * g - i + 2), m8)
        m_row = [jnp.max(m, axis=0, keepdims=True) for m in m8]

        def pv_tile(g, l8):
            off = pl.multiple_of(g * KEY_TILE, KEY_TILE)
            vt_t = vt_s[:, pl.ds(off, KEY_TILE)]
            out = []
            for c in range(n_dots):
                ps = []
                for s in range(HEADS_PER_DOT):
                    h = c * HEADS_PER_DOT + s
                    p = jnp.exp2(l_s[h, pl.ds(off, KEY_TILE), :] - m_row[h])
                    out.append(l8[h] + _fold8(p, jnp.add))
                    ps.append(p.astype(BF16))
                acc_s[c] += jnp.dot(vt_t, jnp.concatenate(ps, axis=1), preferred_element_type=F32)
            return tuple(out)

        return lax.fori_loop(0, n_att, pv_tile, l8_zero)

    l8 = lax.cond(safe, single_pass, two_pass)

    for c in range(n_dots):
        acc = acc_s[c]
        for s in range(HEADS_PER_DOT):
            h = c * HEADS_PER_DOT + s
            denom = jnp.sum(l8[h], axis=0, keepdims=True)
            ot_s[h * HEAD_DIM:(h + 1) * HEAD_DIM, :] = acc[:, s * Q_BLOCK:(s + 1) * Q_BLOCK] / denom
    o_ref[...] = ot_s[...].T.astype(o_ref.dtype)


def _attention(zs, zb, near_bias, aux, batch, seq, n_heads, qi_block):
    d_q = n_heads * HEAD_DIM
    nqb = seq // Q_BLOCK
    k_top = min(INDEX_TOPK, seq // 4)
    assert seq % COUNT_TILE == 0 and n_heads % HEADS_PER_DOT == 0
    assert n_heads & (n_heads - 1) == 0 and N_BIAS_LANES * n_heads <= LANES - HEAD_DIM
    hsel = (np.arange(d_q)[:, None] // HEAD_DIM == np.arange(LANES)[None, :]).astype(np.float32)
    kern = functools.partial(_attn_kernel, k_top=k_top, n_heads=n_heads,
                             pos_bits=max(1, (seq - 1).bit_length()))
    return pl.pallas_call(
        kern,
        grid=(batch, nqb),
        in_specs=[
            pl.BlockSpec((seq, 2 * LANES), lambda b, i: (b, 0)),
            pl.BlockSpec((Q_BLOCK, d_q), lambda b, i: (b * nqb + i, 0)),
            pl.BlockSpec((Q_BLOCK, IDX_HEADS * IDX_DIM), lambda b, i: (b * nqb + i, qi_block)),
            pl.BlockSpec(aux.shape, lambda b, i: (0, 0)),
            pl.BlockSpec(hsel.shape, lambda b, i: (0, 0)),
            pl.BlockSpec(near_bias.shape, lambda b, i: (0, 0, 0, 0)),
        ],
        out_specs=pl.BlockSpec((Q_BLOCK, d_q), lambda b, i: (b * nqb + i, 0)),
        out_shape=jax.ShapeDtypeStruct((batch * seq, d_q), BF16),
        scratch_shapes=[
            pltpu.VMEM((seq, LANES), BF16),
            pltpu.VMEM((seq, IDX_DIM), BF16),
            pltpu.VMEM((KV_DIM, seq), BF16),
            pltpu.VMEM((n_heads * Q_BLOCK, LANES), BF16),
            pltpu.VMEM((IDX_HEADS * Q_BLOCK, IDX_DIM), BF16),
            pltpu.VMEM((seq, Q_BLOCK), I32),
            pltpu.VMEM((seq, Q_BLOCK), F32),
            pltpu.VMEM((n_heads, seq, Q_BLOCK), F32),
            pltpu.VMEM((n_heads // HEADS_PER_DOT, KV_DIM, HEADS_PER_DOT * Q_BLOCK), F32),
            pltpu.VMEM((d_q, Q_BLOCK), F32),
            pltpu.SMEM((1,), F32),
        ],
        compiler_params=_params(2),
        name="dsa_attention",
    )(zs, zb, zb, aux, jnp.asarray(hsel, BF16), near_bias)


def _t5_bucket(rel):
    half = REL_BUCKETS // 2
    max_exact = half // 2
    ret = jnp.where(rel > 0, half, 0)
    n = jnp.abs(rel)
    nf = jnp.maximum(n, 1).astype(F32)
    large = max_exact + (jnp.log(nf / max_exact) / math.log(REL_MAX_DIST / max_exact)
                         * (half - max_exact)).astype(I32)
    large = jnp.minimum(large, half - 1)
    return ret + jnp.where(n < max_exact, n, large)


def _bias_tables(rel_bias):
    n_heads = rel_bias.shape[1]
    bias2 = rel_bias.astype(F32) * LOG2E
    far = bias2[_t5_bucket(jnp.asarray(-2 * REL_MAX_DIST, I32))]
    b_max = jnp.max(bias2, axis=0)
    aux = jnp.zeros((SUBLANES, LANES), F32).at[0:3, :n_heads].set(
        jnp.stack([far, b_max, b_max - jnp.min(bias2, axis=0)]))
    rel_min = -2 * Q_BLOCK - (Q_BLOCK - 1)
    by_rel = bias2[_t5_bucket(jnp.arange(rel_min, KEY_TILE, dtype=I32))] - far[None, :]
    span = KEY_TILE + Q_BLOCK - 1
    tiles = []
    for d in range(3):
        w = lax.slice_in_dim(by_rel, d * Q_BLOCK, d * Q_BLOCK + span, axis=0)
        wrapped = jnp.tile(jnp.concatenate([w, jnp.zeros((1, n_heads), F32)]), (Q_BLOCK, 1))
        rows = wrapped[:Q_BLOCK * span].reshape(Q_BLOCK, span, n_heads)
        tiles.append(rows[:, Q_BLOCK - 1:Q_BLOCK - 1 + KEY_TILE, :])
    near = jnp.transpose(jnp.stack(tiles), (0, 3, 2, 1))
    return near, aux


def _one_minus_sq(a, log_a):
    y = 2.0 * log_a
    series = -y * (1.0 + y * (1.0 / 2 + y * (1.0 / 6)))
    return jnp.where(y > -0.03, series, 1.0 - a * a)


def _rglru_kernel(xr_ref, yr_ref, sh_ref, cw_ref, cb_ref, wa_ref, ba_ref, wx_ref, bx_ref, lam_ref,
                  o_ref, ext_s, xc_s, a_s, u_s, h_s, *, ts, bw):
    j = pl.program_id(1)
    width = xc_s.shape[1]

    @pl.when(j == 0)
    def _():
        ext_s[0:SUBLANES, :] = jnp.zeros((SUBLANES, width), F32)
        h_s[...] = jnp.zeros(h_s.shape, F32)

    x16 = xr_ref[...]
    x = x16.astype(F32)
    taps = [jnp.dot(sh_ref[k], x16, preferred_element_type=F32) for k in range(CONV_W - 1)]
    xc = taps[0] * cw_ref[0:1, :]
    for k in range(1, CONV_W - 1):
        xc = xc + taps[k] * cw_ref[k:k + 1, :]
    xc_s[...] = xc + x * cw_ref[CONV_W - 1:CONV_W, :] + cb_ref[...]
    ext_s[SUBLANES:2 * SUBLANES, :] = x[0:SUBLANES, :]
    head = ext_s[SUBLANES - 3:2 * SUBLANES - 3, :] * cw_ref[0:1, :]
    for k in range(1, CONV_W):
        head = head + ext_s[SUBLANES - 3 + k:2 * SUBLANES - 3 + k, :] * cw_ref[k:k + 1, :]
    xc_s[0:SUBLANES, :] = head + cb_ref[...]
    ext_s[0:SUBLANES, :] = x[ts - SUBLANES:ts, :]

    sp = jax.nn.softplus(-lam_ref[...])
    for n in range(RNN_BLOCKS):
        cs = slice(n * bw, (n + 1) * bw)
        xb = xc_s[:, cs]
        xb16 = xb.astype(BF16)
        r = jax.nn.sigmoid(jnp.dot(xb16, wa_ref[n], preferred_element_type=F32) + ba_ref[:, cs])
        g = jax.nn.sigmoid(jnp.dot(xb16, wx_ref[n], preferred_element_type=F32) + bx_ref[:, cs])
        log_a = -LRU_C * r * sp[:, cs]
        a = jnp.exp(log_a)
        a_s[:, cs] = a
        u_s[:, cs] = jnp.sqrt(_one_minus_sq(a, log_a)) * (g * xb)

    row8 = lax.broadcasted_iota(I32, (SUBLANES, width), 0)

    def scan_rows(blk, h_prev):
        r0 = pl.multiple_of(blk * SUBLANES, SUBLANES)
        a_cum = a_s[pl.ds(r0, SUBLANES), :]
        u_cum = u_s[pl.ds(r0, SUBLANES), :]
        d = 1
        while d < SUBLANES:
            keep = row8 >= d
            u_cum = jnp.where(keep, a_cum * pltpu.roll(u_cum, d, 0) + u_cum, u_cum)
            a_cum = jnp.where(keep, a_cum * pltpu.roll(a_cum, d, 0), a_cum)
            d *= 2
        h = a_cum * h_prev + u_cum
        u_s[pl.ds(r0, SUBLANES), :] = h
        return h[SUBLANES - 1:SUBLANES, :]

    h_s[...] = lax.fori_loop(0, ts // SUBLANES, scan_rows, h_s[...], unroll=4)
    o_ref[...] = (u_s[...] * jax.nn.gelu(yr_ref[...].astype(F32))).astype(o_ref.dtype)


def _rglru(zb, conv_w, conv_b, wa, ba, wx, bx, lam, batch, seq, width, xr_block, yr_block, ts):
    nts = seq // ts
    bw = width // RNN_BLOCKS
    row = lambda v: v.reshape(1, width)
    full = lambda shape: pl.BlockSpec(shape, lambda b, j: (0,) * len(shape))
    t_out, t_in = np.arange(ts)[:, None], np.arange(ts)[None, :]
    shifts = np.stack([(t_out - t_in == CONV_W - 1 - k) for k in range(CONV_W - 1)]).astype(np.float32)
    return pl.pallas_call(
        functools.partial(_rglru_kernel, ts=ts, bw=bw),
        grid=(batch, nts),
        in_specs=[pl.BlockSpec((ts, width), lambda b, j: (b * nts + j, xr_block)),
                  pl.BlockSpec((ts, width), lambda b, j: (b * nts + j, yr_block)),
                  full((CONV_W - 1, ts, ts)),
                  full((CONV_W, width)), full((1, width)),
                  full((RNN_BLOCKS, bw, bw)), full((1, width)),
                  full((RNN_BLOCKS, bw, bw)), full((1, width)), full((1, width))],
        out_specs=pl.BlockSpec((ts, width), lambda b, j: (b * nts + j, 0)),
        out_shape=jax.ShapeDtypeStruct((batch * seq, width), BF16),
        scratch_shapes=[pltpu.VMEM((2 * SUBLANES, width), F32),
                        pltpu.VMEM((ts, width), F32),
                        pltpu.VMEM((ts, width), F32),
                        pltpu.VMEM((ts, width), F32),
                        pltpu.VMEM((1, width), F32)],
        compiler_params=_params(2),
        name="rglru",
    )(zb, zb, jnp.asarray(shifts, BF16), conv_w, row(conv_b), wa.astype(BF16), row(ba),
      wx.astype(BF16), row(bx), row(lam))


def _layer_norm(y, g, b):
    mu = jnp.mean(y, axis=-1, keepdims=True)
    var = jnp.mean(jnp.square(y - mu), axis=-1, keepdims=True)
    return (y - mu) * lax.rsqrt(var + LN_EPS) * g + b


def _top2(lg):
    lane = lax.broadcasted_iota(I32, lg.shape, 1)
    v1 = jnp.max(lg, axis=1, keepdims=True)
    i1 = jnp.min(jnp.where(lg == v1, lane, LANES), axis=1, keepdims=True)
    lg2 = jnp.where(lane == i1, -jnp.inf, lg)
    v2 = jnp.max(lg2, axis=1, keepdims=True)
    i2 = jnp.min(jnp.where(lg2 == v2, lane, LANES), axis=1, keepdims=True)
    e2 = jnp.exp(v2 - v1)
    denom = 1.0 + e2
    idx = jnp.where(lane == 0, i1, jnp.where(lane == 1, i2, 0))
    gates = jnp.where(lane == 0, 1.0 / denom, jnp.where(lane == 1, e2 / denom, 0.0))
    return idx, gates


def _merge_kernel(oa_ref, or_ref, ga_ref, gr_ref, x_ref, wba_ref, wbr_ref, wo_ref, g_ref, b_ref,
                  *rest, alpha, routed):
    a = jnp.dot(oa_ref[...], wba_ref[...], preferred_element_type=F32)
    r = jnp.dot(or_ref[...], wbr_ref[...], preferred_element_type=F32)
    merged = (jax.nn.sigmoid(ga_ref[...].astype(F32)) * a
              + jax.nn.sigmoid(gr_ref[...].astype(F32)) * r)
    mix = jnp.dot(merged.astype(BF16), wo_ref[...], preferred_element_type=F32)
    x1 = _layer_norm(alpha * x_ref[...] + mix, g_ref[...], b_ref[...])
    x1b = x1.astype(BF16)
    if routed:
        rw_ref, rb_ref, x1_ref, x1b_ref, idx_ref, gate_ref = rest
        idx_ref[...], gate_ref[...] = _top2(
            jnp.dot(x1b, rw_ref[...], preferred_element_type=F32) + rb_ref[...])
    else:
        x1_ref, x1b_ref = rest
    x1_ref[...] = x1
    x1b_ref[...] = x1b


def _merge_ln(oa, orn, zb, x, wba, wbr, wo, g, b, alpha, ga_block, gr_block, tm, router=None):
    m, d = x.shape
    tok = lambda blk: pl.BlockSpec((tm, d), lambda i: (i, blk))
    full = lambda shape: pl.BlockSpec(shape, lambda i: (0,) * len(shape))
    lanes = pl.BlockSpec((tm, LANES), lambda i: (i, 0))
    args = [oa, orn, zb, zb, x, wba, wbr, wo, g.reshape(1, d), b.reshape(1, d)]
    in_specs = [tok(0), tok(0), tok(ga_block), tok(gr_block), tok(0),
                full((d, d)), full((d, d)), full((d, d)), full((1, d)), full((1, d))]
    out_specs = [tok(0), tok(0)]
    out_shape = [jax.ShapeDtypeStruct((m, d), F32), jax.ShapeDtypeStruct((m, d), BF16)]
    if router is not None:
        rw = jnp.zeros((d, LANES), BF16).at[:, :N_EXPERTS].set(router[0].astype(BF16))
        rb = jnp.full((1, LANES), -jnp.inf, F32).at[0, :N_EXPERTS].set(router[1].astype(F32))
        args += [rw, rb]
        in_specs += [full((d, LANES)), full((1, LANES))]
        out_specs += [lanes, lanes]
        out_shape += [jax.ShapeDtypeStruct((m, LANES), I32), jax.ShapeDtypeStruct((m, LANES), F32)]
    return pl.pallas_call(
        functools.partial(_merge_kernel, alpha=alpha, routed=router is not None),
        grid=(m // tm,),
        in_specs=in_specs, out_specs=out_specs, out_shape=out_shape,
        compiler_params=_params(1),
        name="merge_ln1",
    )(*args)


def _ple_kernel(*refs, alpha, routed):
    if routed:
        (x1_ref, x1b_ref, p_ref, pg_ref, pp_ref, g_ref, b_ref,
         ya_ref, yb_ref, gt_ref, x2_ref, x2b_ref) = refs
        f = (gt_ref[:, 0:1] * ya_ref[...].astype(F32)
             + gt_ref[:, 1:2] * yb_ref[...].astype(F32))
    else:
        x1_ref, x1b_ref, p_ref, pg_ref, pp_ref, g_ref, b_ref, f_ref, x2_ref, x2b_ref = refs
        f = f_ref[...].astype(F32)
    gate = jax.nn.sigmoid(jnp.dot(x1b_ref[...], pg_ref[...], preferred_element_type=F32))
    proj = jnp.dot(p_ref[...].astype(BF16), pp_ref[...], preferred_element_type=F32)
    x2 = _layer_norm(alpha * x1_ref[...] + f + gate * proj, g_ref[...], b_ref[...])
    x2_ref[...] = x2
    x2b_ref[...] = x2.astype(BF16)


def _ple_ln(x1, x1b, p_all, layer, pg, pp, g, b, f_parts, alpha, tm):
    m, d = x1.shape
    pd = p_all.shape[1]
    nt = m // tm
    tok = pl.BlockSpec((tm, d), lambda i: (i, 0))
    full = lambda shape: pl.BlockSpec(shape, lambda i: (0,) * len(shape))
    routed = len(f_parts) == 3
    f_specs = [tok, tok, pl.BlockSpec((tm, LANES), lambda i: (i, 0))] if routed else [tok]
    return pl.pallas_call(
        functools.partial(_ple_kernel, alpha=alpha, routed=routed),
        grid=(nt,),
        in_specs=[tok, tok, pl.BlockSpec((tm, pd), lambda i: (layer * nt + i, 0)),
                  full((d, d)), full((pd, d)), full((1, d)), full((1, d))] + f_specs,
        out_specs=[tok, tok],
        out_shape=[jax.ShapeDtypeStruct((m, d), F32), jax.ShapeDtypeStruct((m, d), BF16)],
        compiler_params=_params(1),
        name="ple_ln2",
    )(x1, x1b, p_all, pg, pp, g.reshape(1, d), b.reshape(1, d), *f_parts)


def _ffn_kernel(te_ref, nt_ref, x_ref, wg_ref, wu_ref, wd_ref, o_ref, acc_s):
    t = pl.program_id(0)
    j = pl.program_id(1)
    last = pl.num_programs(1) - 1
    live = t < nt_ref[0]

    @pl.when(live)
    def _():
        wg = wg_ref[0].astype(BF16)
        wu = wu_ref[0].astype(BF16)
        wd = wd_ref[0].astype(BF16)
        x = x_ref[...]
        gate = jnp.dot(x, wg, preferred_element_type=F32)
        up = jnp.dot(x, wu, preferred_element_type=F32)
        hid = (jax.nn.silu(gate) * up).astype(BF16)
        cw = wd.shape[1] // DOWN_SPLIT
        for c in range(DOWN_SPLIT):
            cols = slice(c * cw, (c + 1) * cw)
            part = jnp.dot(hid, wd[:, cols], preferred_element_type=F32)
            acc_s[:, cols] = jnp.where(j == 0, part, acc_s[:, cols] + part)

    @pl.when(j == last)
    def _():
        o_ref[...] = jnp.where(live, acc_s[...], 0.0).astype(o_ref.dtype)


def _ffn(x, tile_expert, n_live, wg, wu, wd, tm, tf):
    rows, d = x.shape
    f = wg.shape[2]
    fcol = lambda t, j, te, nt: jnp.where(t < nt[0], j, 0)
    resident = dict(pipeline_mode=pl.Buffered(1)) if (wg.shape[0] == 1 and tf == f) else {}
    grid_spec = pltpu.PrefetchScalarGridSpec(
        num_scalar_prefetch=2,
        grid=(rows // tm, f // tf),
        in_specs=[pl.BlockSpec((tm, d), lambda t, j, te, nt: (t, 0)),
                  pl.BlockSpec((1, d, tf), lambda t, j, te, nt: (te[t], 0, fcol(t, j, te, nt)),
                               **resident),
                  pl.BlockSpec((1, d, tf), lambda t, j, te, nt: (te[t], 0, fcol(t, j, te, nt)),
                               **resident),
                  pl.BlockSpec((1, tf, d), lambda t, j, te, nt: (te[t], fcol(t, j, te, nt), 0),
                               **resident)],
        out_specs=pl.BlockSpec((tm, d), lambda t, j, te, nt: (t, 0)),
        scratch_shapes=[pltpu.VMEM((tm, d), F32)])
    return pl.pallas_call(
        _ffn_kernel, grid_spec=grid_spec,
        out_shape=jax.ShapeDtypeStruct((rows, d), BF16),
        compiler_params=_params(2),
        name="swiglu_ffn",
    )(tile_expert, n_live, x, wg, wu, wd)


def _dispatch_plan(top_i, tm):
    m = top_i.shape[0]
    n_assign = m * TOP_K
    e_flat = top_i.reshape(n_assign)
    onehot = (e_flat[:, None] == jnp.arange(N_EXPERTS, dtype=I32)[None, :]).astype(I32)
    csum = jnp.cumsum(onehot, axis=0)
    counts = csum[-1]
    rank = jnp.sum(csum * onehot, axis=1) - 1
    padded = ((counts + tm - 1) // tm) * tm
    ends = jnp.cumsum(padded)
    pos = (ends - padded)[e_flat] + rank
    n_rows = n_assign + N_EXPERTS * tm
    src = jnp.zeros((n_rows,), I32).at[pos].set(jnp.arange(n_assign, dtype=I32) // TOP_K)
    tile_start = jnp.arange(n_rows // tm, dtype=I32) * tm
    tile_expert = jnp.minimum(jnp.searchsorted(ends, tile_start, side="right"),
                              N_EXPERTS - 1).astype(I32)
    n_live = (ends[-1] // tm).astype(I32).reshape(1)
    return src, pos.reshape(m, TOP_K), tile_expert, n_live


def _pick(n, candidates):
    for c in candidates:
        if n % c == 0:
            return c
    return n


def _forward(x, p, w_in, conv_w, conv_b, lru_wa, lru_ba, lru_wx, lru_bx, lru_lam, w_br_attn,
             w_br_rnn, w_o, rel_bias, ln1_g, ln1_b, ffn_w_gate, ffn_w_up, ffn_w_down, moe_router,
             moe_router_b, moe_w_gate, moe_w_up, moe_w_down, ple_w_gate, ple_w_proj, ln2_g, ln2_b):
    batch, seq, d = x.shape
    depth = w_in.shape[0]
    m = batch * seq
    n_heads = d // HEAD_DIM
    alpha = (2 * depth) ** 0.25
    d_q, d_qi = n_heads * HEAD_DIM, IDX_HEADS * IDX_DIM
    assert d_q == d and w_br_rnn.shape[1] == d and (d_q + 4 * d) % d_qi == 0

    sizes = (d_q, KV_DIM, KV_DIM, d_qi, IDX_DIM, IDX_HEADS, d, d, d, d)
    offs = np.concatenate([[0], np.cumsum(sizes)])
    col = lambda w, k: w[:, offs[k]:offs[k + 1]]
    q_scale = HEAD_DIM ** -0.5 * LOG2E
    big_scale = jnp.concatenate([jnp.full((1, d_q), q_scale, F32),
                                 jnp.ones((1, 4 * d + d_qi), F32)], axis=1)
    small_scale = jnp.ones((1, 2 * LANES), F32)
    xr_block, yr_block, ga_block, gr_block = 1, 2, 3, 4
    qi_block = (d_q + 4 * d) // d_qi

    near_bias, attn_aux = _bias_tables(rel_bias)

    tm = _pick(m, (1024, 512, 256, 128))
    tn_big = _pick(d_q + 4 * d + d_qi, (512, 256, 128))
    ts = _pick(seq, (256, 128))
    tm_res = _pick(m, (512, 256, 128))

    xf = x.reshape(m, d)
    xb = xf.astype(BF16)
    p_all = p.reshape(depth * m, p.shape[-1])

    for i in range(depth):
        w = w_in[i]
        w_big = jnp.concatenate([col(w, 0), col(w, 6), col(w, 7), col(w, 8), col(w, 9), col(w, 3)],
                                axis=1).astype(BF16)
        w_small = jnp.concatenate(
            [col(w, 1), col(w, 2), col(w, 4), col(w, 5),
             jnp.zeros((d, 2 * LANES - 2 * KV_DIM - IDX_DIM - IDX_HEADS), w.dtype)], axis=1).astype(BF16)
        zb = _matmul(xb, w_big, big_scale, BF16, _pick(m, (2 * tm, tm)), tn_big)
        zs = _matmul(xb, w_small, small_scale, F32, tm, 2 * LANES)

        o_attn = _attention(zs, zb, near_bias, attn_aux, batch, seq, n_heads, qi_block)
        o_rnn = _rglru(zb, conv_w[i], conv_b[i], lru_wa[i], lru_ba[i], lru_wx[i], lru_bx[i],
                       lru_lam[i], batch, seq, d, xr_block, yr_block, ts)
        j = i // 2
        router = None if i % 2 == 0 else (moe_router[j], moe_router_b[j])
        merged = _merge_ln(o_attn, o_rnn, zb, xf, w_br_attn[i].astype(BF16),
                           w_br_rnn[i].astype(BF16), w_o[i].astype(BF16), ln1_g[i], ln1_b[i],
                           alpha, ga_block, gr_block, tm_res, router)
        x1, x1b = merged[0], merged[1]

        if i % 2 == 0:
            f_dim = ffn_w_gate.shape[-1]
            n_tiles = m // tm_res
            f_parts = [_ffn(x1b, jnp.zeros((n_tiles,), I32), jnp.full((1,), n_tiles, I32),
                            ffn_w_gate[j][None].astype(BF16), ffn_w_up[j][None].astype(BF16),
                            ffn_w_down[j][None].astype(BF16), tm_res, f_dim)]
        else:
            f_dim = moe_w_gate.shape[-1]
            tf = _pick(f_dim, (512, f_dim // 2))
            idx, gates = merged[2], merged[3]
            src, pos, tile_expert, n_live = _dispatch_plan(idx[:, :TOP_K], tm)
            all_experts = lambda w: w.reshape((-1,) + w.shape[2:])
            y = _ffn(x1b[src], tile_expert + j * N_EXPERTS, n_live, all_experts(moe_w_gate),
                     all_experts(moe_w_up), all_experts(moe_w_down), tm, tf)
            f_parts = [y[pos[:, 0]], y[pos[:, 1]], gates]

        xf, xb = _ple_ln(x1, x1b, p_all, i, ple_w_gate[i].astype(BF16), ple_w_proj[i].astype(BF16),
                         ln2_g[i], ln2_b[i], f_parts, alpha, tm_res)

    return xf.reshape(batch, seq, d)


@jax.jit
def kernel(x, p, w_in, conv_w, conv_b, lru_wa, lru_ba, lru_wx, lru_bx, lru_lam, w_br_attn, w_br_rnn,
           w_o, rel_bias, ln1_g, ln1_b, ffn_w_gate, ffn_w_up, ffn_w_down, moe_router, moe_router_b,
           moe_w_gate, moe_w_up, moe_w_down, ple_w_gate, ple_w_proj, ln2_g, ln2_b):
    return _forward(x, p, w_in, conv_w, conv_b, lru_wa, lru_ba, lru_wx, lru_bx, lru_lam, w_br_attn,
                    w_br_rnn, w_o, rel_bias, ln1_g, ln1_b, ffn_w_gate, ffn_w_up, ffn_w_down,
                    moe_router, moe_router_b, moe_w_gate, moe_w_up, moe_w_down, ple_w_gate,
                    ple_w_proj, ln2_g, ln2_b)
```
